```python
import math
import jax
import jax.numpy as jnp
from jax import lax
import numpy as np

D_MODEL = 1024
BATCH = 8
SEQ = 2048
DEPTH = 1

N_HEADS = 16
HEAD_DIM = 64
ATTN_WIDTH = N_HEADS * HEAD_DIM
MOBA_BLOCK = 256
MOBA_TOPK = 3
Q_CHUNK = 16
SSD_EXPAND = 2
D_INNER = SSD_EXPAND * D_MODEL
SSD_HEAD_DIM = 64
SSD_HEADS = D_INNER // SSD_HEAD_DIM
N_GROUPS = 4
HEADS_PER_GROUP = SSD_HEADS // N_GROUPS
D_STATE = 128
CONV_K = 4
SSD_CHUNK = 128
D_XBC = D_INNER + 2 * N_GROUPS * D_STATE
N_BRANCHES = 2
COL_Q = ATTN_WIDTH
COL_K = 2 * ATTN_WIDTH
COL_V = 3 * ATTN_WIDTH
COL_Z = COL_V + D_INNER
COL_XBC = COL_Z + D_XBC
COL_DT = COL_XBC + SSD_HEADS
N_IN = COL_DT + N_BRANCHES * D_MODEL
N_EXPERTS = 32
TOP_K = 4
D_FF = D_MODEL
SWIGLU_LIMIT = 7.0
SWIGLU_ALPHA = 1.702
MOE_BLOCK = 256
NORM_EPS = 1e-6
NEG_INF = -1e30

kernel_name = 'hybrid_moba_ssd_gated_moe'


def rms_norm(x, w):
    xf = x.astype(jnp.float32)
    y = xf * lax.rsqrt(jnp.mean(xf * xf, axis=-1, keepdims=True) + NORM_EPS)
    return (y * w.astype(jnp.float32)).astype(x.dtype)


def moba_attention(q, k, v):
    b, l, h, d = q.shape
    lp = -(-l // MOBA_BLOCK) * MOBA_BLOCK
    pad = ((0, 0), (0, lp - l), (0, 0), (0, 0))
    nb = lp // MOBA_BLOCK
    q = jnp.pad(q, pad).transpose(0, 2, 1, 3)
    kb = jnp.pad(k, pad).transpose(0, 2, 1, 3).reshape(b, h, nb, MOBA_BLOCK, d)
    vb = jnp.pad(v, pad).transpose(0, 2, 1, 3).reshape(b, h, nb, MOBA_BLOCK, d)
    k_mean = jnp.mean(kb.astype(jnp.float32), axis=3)
    blk_scores = jnp.einsum('bhtd,bhnd->bhtn', q.astype(jnp.float32), k_mean)
    q_blk = jnp.arange(lp) // MOBA_BLOCK
    past = jnp.arange(nb)[None, :] < q_blk[:, None]
    blk_scores = jnp.where(past, blk_scores, NEG_INF)
    n_sel = min(MOBA_TOPK, nb)
    _, sel = lax.top_k(blk_scores, n_sel)
    sel_valid = sel < q_blk[:, None]
    nqc = lp // Q_CHUNK

    def to_chunks(t):
        return jnp.moveaxis(t.reshape(b, h, nqc, Q_CHUNK, t.shape[-1]), 2, 0)

    gather_blocks = jax.vmap(jax.vmap(lambda blocks, idx: blocks[idx]))
    scale = HEAD_DIM ** -0.5

    def attend_chunk(args):
        c, q_c, sel_c, valid_c = args
        blk = (c * Q_CHUNK) // MOBA_BLOCK
        k_own = lax.dynamic_index_in_dim(kb, blk, axis=2, keepdims=False)
        v_own = lax.dynamic_index_in_dim(vb, blk, axis=2, keepdims=False)
        q_pos = c * Q_CHUNK + jnp.arange(Q_CHUNK)
        k_pos = blk * MOBA_BLOCK + jnp.arange(MOBA_BLOCK)
        s_own = jnp.einsum('bhqd,bhkd->bhqk', q_c, k_own).astype(jnp.float32) * scale
        s_own = jnp.where(k_pos[None, :] <= q_pos[:, None], s_own, NEG_INF)
        k_sel = gather_blocks(kb, sel_c)
        v_sel = gather_blocks(vb, sel_c)
        s_sel = jnp.einsum('bhqd,bhqnkd->bhqnk', q_c, k_sel).astype(jnp.float32) * scale
        s_sel = jnp.where(valid_c[..., None], s_sel, NEG_INF)
        logits = jnp.concatenate([s_own, s_sel.reshape(b, h, Q_CHUNK, n_sel * MOBA_BLOCK)], axis=-1)
        p = jax.nn.softmax(logits, axis=-1).astype(v_own.dtype)
        p_own = p[..., :MOBA_BLOCK]
        p_sel = p[..., MOBA_BLOCK:].reshape(b, h, Q_CHUNK, n_sel, MOBA_BLOCK)
        return (jnp.einsum('bhqk,bhkd->bhqd', p_own, v_own)
                + jnp.einsum('bhqnk,bhqnkd->bhqd', p_sel, v_sel))

    out = lax.map(attend_chunk, (jnp.arange(nqc), to_chunks(q), to_chunks(sel), to_chunks(sel_valid)))
    out = out.transpose(1, 0, 3, 2, 4).reshape(b, lp, h, d)[:, :l]
    return out.reshape(b, l, h * d)


def segsum(x):
    t = x.shape[-1]
    cs = jnp.cumsum(x, axis=-1)
    diff = cs[..., :, None] - cs[..., None, :]
    return jnp.where(jnp.tril(jnp.ones((t, t), dtype=bool)), diff, -jnp.inf)


def ssd_chunked(xdt, adt, bm, cm):
    b, l, g, r, p = xdt.shape
    n = bm.shape[-1]
    nc = l // SSD_CHUNK
    xc = xdt.reshape(b, nc, SSD_CHUNK, g, r, p)
    ac = adt.reshape(b, nc, SSD_CHUNK, g, r).transpose(0, 3, 4, 1, 2)
    bc = bm.reshape(b, nc, SSD_CHUNK, g, n)
    cc = cm.reshape(b, nc, SSD_CHUNK, g, n)
    a_cum = jnp.cumsum(ac, axis=-1)
    lmat = jnp.exp(segsum(ac))
    cb = jnp.einsum('bclgn,bcsgn->bgcls', cc, bc)
    y_diag = jnp.einsum('bgrcls,bcsgrp->bclgrp', cb[:, :, None] * lmat, xc)
    decay_states = jnp.exp(a_cum[..., -1:] - a_cum).transpose(0, 3, 4, 1, 2)
    states = jnp.einsum('bclgn,bclgrp->bcgrpn', bc, xc * decay_states[..., None])
    states = jnp.concatenate([jnp.zeros_like(states[:, :1]), states], axis=1)
    chunk_decay = jnp.exp(segsum(jnp.pad(a_cum[..., -1], ((0, 0), (0, 0), (0, 0), (1, 0)))))
    states = jnp.einsum('bgrzc,bcgrpn->bzgrpn', chunk_decay, states)[:, :-1]
    state_decay_out = jnp.exp(a_cum).transpose(0, 3, 4, 1, 2)
    y_off = jnp.einsum('bclgn,bcgrpn->bclgrp', cc, states) * state_decay_out[..., None]
    return (y_diag + y_off).reshape(b, l, g, r, p)


def ssd_mixer(z, xbc, dt_raw, conv_w, conv_b, dt_bias, a_log, d_skip, norm_w):
    b, l, _ = xbc.shape
    xbc = lax.conv_general_dilated(
        xbc, conv_w[:, None, :].astype(xbc.dtype), window_strides=(1,),
        padding=[(CONV_K - 1, 0)], dimension_numbers=('NWC', 'WIO', 'NWC'),
        feature_group_count=D_XBC)
    xbc = jax.nn.silu(xbc + conv_b)
    xs = xbc[..., :D_INNER].reshape(b, l, N_GROUPS, HEADS_PER_GROUP, SSD_HEAD_DIM).astype(jnp.float32)
    bm = xbc[..., D_INNER:D_INNER + N_GROUPS * D_STATE].reshape(b, l, N_GROUPS, D_STATE).astype(jnp.float32)
    cm = xbc[..., D_INNER + N_GROUPS * D_STATE:].reshape(b, l, N_GROUPS, D_STATE).astype(jnp.float32)
    dt = jax.nn.softplus(dt_raw.astype(jnp.float32) + dt_bias.astype(jnp.float32))
    dt = dt.reshape(b, l, N_GROUPS, HEADS_PER_GROUP)
    a = -jnp.exp(a_log.astype(jnp.float32)).reshape(N_GROUPS, HEADS_PER_GROUP)
    y = ssd_chunked(xs * dt[..., None], dt * a, bm, cm)
    y = y + d_skip.astype(jnp.float32).reshape(N_GROUPS, HEADS_PER_GROUP, 1) * xs
    gated = y * jax.nn.silu(z.astype(jnp.float32)).reshape(b, l, N_GROUPS, HEADS_PER_GROUP, SSD_HEAD_DIM)
    gated = gated.reshape(b, l, N_GROUPS, HEADS_PER_GROUP * SSD_HEAD_DIM)
    normed = rms_norm(gated, norm_w.reshape(N_GROUPS, HEADS_PER_GROUP * SSD_HEAD_DIM))
    return normed.reshape(b, l, D_INNER).astype(z.dtype)


def moe_ffn(h, w_router, b_router, w_gate_up, b_gate_up, w_down, b_down):
    t, d = h.shape
    logits = jnp.matmul(h, w_router).astype(jnp.float32) + b_router.astype(jnp.float32)
    top_vals, top_idx = lax.top_k(logits, TOP_K)
    gates = jax.nn.softmax(top_vals, axis=-1)
    n_assign = t * TOP_K
    e_flat = top_idx.reshape(-1)
    tok_flat = jnp.repeat(jnp.arange(t, dtype=jnp.int32), TOP_K)
    w_flat = gates.reshape(-1)
    order = jnp.argsort(e_flat)
    e_sorted = e_flat[order]
    counts = jnp.bincount(e_flat, length=N_EXPERTS)
    padded = ((counts + MOE_BLOCK - 1) // MOE_BLOCK) * MOE_BLOCK
    start = jnp.cumsum(counts) - counts
    cum_padded = jnp.cumsum(padded)
    pstart = cum_padded - padded
    dest = pstart[e_sorted] + (jnp.arange(n_assign) - start[e_sorted])
    n_rows = n_assign + N_EXPERTS * MOE_BLOCK
    n_blocks = n_rows // MOE_BLOCK
    buf_tok = jnp.zeros((n_rows,), jnp.int32).at[dest].set(tok_flat[order])
    buf_w = jnp.zeros((n_rows,), jnp.float32).at[dest].set(w_flat[order])
    block_expert = jnp.minimum(
        jnp.searchsorted(cum_padded, jnp.arange(n_blocks) * MOE_BLOCK, side='right'), N_EXPERTS - 1)
    xb = h[buf_tok].reshape(n_blocks, MOE_BLOCK, d)

    def expert_block(args):
        xblk, e = args
        gu = jnp.matmul(xblk, w_gate_up[e]) + b_gate_up[e]
        g = jnp.minimum(gu[:, 0::2], SWIGLU_LIMIT)
        u = jnp.clip(gu[:, 1::2], -SWIGLU_LIMIT, SWIGLU_LIMIT)
        act = (u + 1.0) * (g * jax.nn.sigmoid(SWIGLU_ALPHA * g))
        return jnp.matmul(act, w_down[e]) + b_down[e]

    yb = lax.map(expert_block, (xb, block_expert)).reshape(n_rows, d)
    return jnp.zeros((t, d), h.dtype).at[buf_tok].add(yb * buf_w[:, None].astype(h.dtype))


def setup_inputs(seed: int = 0) -> dict:
    key = jax.random.key(seed)
    ks = jax.random.split(key, 24)

    def nrm(k, shape, scale):
        return jax.random.normal(k, shape, jnp.float32) * scale

    dt0 = jnp.exp(jax.random.uniform(ks[8], (DEPTH, SSD_HEADS), jnp.float32,
                                     minval=math.log(1e-3), maxval=math.log(1e-1)))
    return {
        'x': nrm(ks[0], (BATCH, SEQ, D_MODEL), 1.0),
        'ln1_w': 1.0 + nrm(ks[1], (DEPTH, D_MODEL), 0.01),
        'w_in': nrm(ks[2], (DEPTH, D_MODEL, N_IN), D_MODEL ** -0.5),
        'gate_b': nrm(ks[3], (DEPTH, N_BRANCHES, D_MODEL), 0.01),
        'q_norm_w': 1.0 + nrm(ks[4], (DEPTH, HEAD_DIM), 0.01),
        'k_norm_w': 1.0 + nrm(ks[5], (DEPTH, HEAD_DIM), 0.01),
        'conv_w': nrm(ks[6], (DEPTH, CONV_K, D_XBC), CONV_K ** -0.5),
        'conv_b': nrm(ks[7], (DEPTH, D_XBC), 0.01),
        'dt_bias': dt0 + jnp.log(-jnp.expm1(-dt0)),
        'a_log': jnp.log(jax.random.uniform(ks[9], (DEPTH, SSD_HEADS), jnp.float32, minval=1.0, maxval=16.0)),
        'd_skip': 1.0 + nrm(ks[10], (DEPTH, SSD_HEADS), 0.01),
        'ssd_norm_w': 1.0 + nrm(ks[11], (DEPTH, D_INNER), 0.01),
        'w_o_attn': nrm(ks[12], (DEPTH, ATTN_WIDTH, D_MODEL), ATTN_WIDTH ** -0.5),
        'w_o_ssd': nrm(ks[13], (DEPTH, D_INNER, D_MODEL), D_INNER ** -0.5),
        'w_out': nrm(ks[14], (DEPTH, D_MODEL, D_MODEL), D_MODEL ** -0.5),
        'ln2_w': 1.0 + nrm(ks[15], (DEPTH, D_MODEL), 0.01),
        'w_router': nrm(ks[16], (DEPTH, D_MODEL, N_EXPERTS), D_MODEL ** -0.5),
        'b_router': nrm(ks[17], (DEPTH, N_EXPERTS), 0.01),
        'w_gate_up': nrm(ks[18], (DEPTH, N_EXPERTS, D_MODEL, 2 * D_FF), D_MODEL ** -0.5),
        'b_gate_up': nrm(ks[19], (DEPTH, N_EXPERTS, 2 * D_FF), 0.01),
        'w_down': nrm(ks[20], (DEPTH, N_EXPERTS, D_FF, D_MODEL), D_FF ** -0.5),
        'b_down': nrm(ks[21], (DEPTH, N_EXPERTS, D_MODEL), 0.01),
    }


def reference(x, ln1_w, w_in, gate_b, q_norm_w, k_norm_w, conv_w, conv_b, dt_bias, a_log,
              d_skip, ssd_norm_w, w_o_attn, w_o_ssd, w_out, ln2_w, w_router, b_router,
              w_gate_up, b_gate_up, w_down, b_down):
    b, l, d = x.shape
    for layer in range(DEPTH):
        h = rms_norm(x, ln1_w[layer])
        proj = jnp.matmul(h, w_in[layer])
        q = proj[..., :COL_Q].reshape(b, l, N_HEADS, HEAD_DIM)
        k = proj[..., COL_Q:COL_K].reshape(b, l, N_HEADS, HEAD_DIM)
        v = proj[..., COL_K:COL_V].reshape(b, l, N_HEADS, HEAD_DIM)
        z = proj[..., COL_V:COL_Z]
        xbc = proj[..., COL_Z:COL_XBC]
        dt_raw = proj[..., COL_XBC:COL_DT]
        gate_logits = proj[..., COL_DT:].reshape(b, l, N_BRANCHES, D_MODEL)
        q = rms_norm(q, q_norm_w[layer])
        k = rms_norm(k, k_norm_w[layer])
        attn = jnp.matmul(moba_attention(q, k, v), w_o_attn[layer])
        ssd = jnp.matmul(ssd_mixer(z, xbc, dt_raw, conv_w[layer], conv_b[layer], dt_bias[layer],
                                   a_log[layer], d_skip[layer], ssd_norm_w[layer]), w_o_ssd[layer])
        gates = jax.nn.sigmoid(gate_logits + gate_b[layer])
        mixed = gates[:, :, 0] * attn + gates[:, :, 1] * ssd
        x = x + jnp.matmul(mixed, w_out[layer])
        h2 = rms_norm(x, ln2_w[layer]).reshape(b * l, d)
        x = x + moe_ffn(h2, w_router[layer], b_router[layer], w_gate_up[layer], b_gate_up[layer],
                        w_down[layer], b_down[layer]).reshape(b, l, d)
    return x
```

```python
import functools

import jax
import jax.numpy as jnp
from jax import lax
from jax.experimental import pallas as pl
from jax.experimental.pallas import tpu as pltpu

F32 = jnp.float32
BF16 = jnp.bfloat16

D_MODEL = 1024
N_HEADS = 16
HEAD_DIM = 64
ATTN_WIDTH = N_HEADS * HEAD_DIM
MOBA_BLOCK = 256
MOBA_TOPK = 3
D_INNER = 2048
SSD_HEAD_DIM = 64
SSD_HEADS = D_INNER // SSD_HEAD_DIM
N_GROUPS = 4
HEADS_PER_GROUP = SSD_HEADS // N_GROUPS
GROUP_WIDTH = HEADS_PER_GROUP * SSD_HEAD_DIM
D_STATE = 128
CONV_K = 4
SSD_CHUNK = 128
D_XBC = D_INNER + 2 * N_GROUPS * D_STATE
N_EXPERTS = 32
TOP_K = 4
D_FF = D_MODEL
SWIGLU_LIMIT = 7.0
SWIGLU_ALPHA = 1.702
MOE_BLOCK = 256
NORM_EPS = 1e-6
NEG_INF = -1e30

LANES = 128
SUBLANES = 8

P_Q = 0
P_Z = 3 * ATTN_WIDTH
P_XBC = P_Z + D_INNER
P_GATE = P_XBC + D_XBC
P_DT = P_GATE + 2 * D_MODEL
NP = P_DT + LANES
PROJ_TN = 1152
VMEM_LIMIT = 56 * 1024 * 1024


def _split3(v):
    hi = v.astype(BF16)
    r1 = v - hi.astype(F32)
    mid = r1.astype(BF16)
    lo = (r1 - mid.astype(F32)).astype(BF16)
    return hi, mid, lo


def _dot(a, b):
    return jnp.dot(a, b, preferred_element_type=F32)


def _dot_nt(a, b):
    return lax.dot_general(a, b, (((1,), (1,)), ((), ())), preferred_element_type=F32)


def _inproj_kernel(x_ref, lnw_ref, w_ref, o_ref, h_ref):
    @pl.when(pl.program_id(1) == 0)
    def _():
        x = x_ref[...]
        ms = jnp.mean(x * x, axis=-1, keepdims=True)
        h_ref[...] = (x * lax.rsqrt(ms + NORM_EPS) * lnw_ref[...]).astype(BF16)

    o_ref[...] = _dot(h_ref[...], w_ref[...]).astype(o_ref.dtype)


def _inproj(x2, lnw, w_r):
    t = x2.shape[0]
    tm = min(1024, t)
    return pl.pallas_call(
        _inproj_kernel,
        grid=(t // tm, NP // PROJ_TN),
        in_specs=[
            pl.BlockSpec((tm, D_MODEL), lambda i, j: (i, 0)),
            pl.BlockSpec((1, D_MODEL), lambda i, j: (0, 0)),
            pl.BlockSpec((D_MODEL, PROJ_TN), lambda i, j: (0, j)),
        ],
        out_specs=pl.BlockSpec((tm, PROJ_TN), lambda i, j: (i, j)),
        out_shape=jax.ShapeDtypeStruct((t, NP), BF16),
        scratch_shapes=[pltpu.VMEM((tm, D_MODEL), BF16)],
        compiler_params=pltpu.CompilerParams(
            dimension_semantics=("parallel", "arbitrary"), vmem_limit_bytes=VMEM_LIMIT),
        name="inproj",
    )(x2, lnw, w_r)


def _moba_kernel(q_ref, k_ref, v_ref, qw_ref, kw_ref, o_ref, *, seq, nb):
    lane = lax.broadcasted_iota(jnp.int32, (1, LANES), 1)
    head0 = lane < HEAD_DIM
    r = lax.broadcasted_iota(jnp.int32, (2 * LANES, LANES), 0) % LANES // HEAD_DIM
    c = lax.broadcasted_iota(jnp.int32, (2 * LANES, LANES), 1) // HEAD_DIM
    avg2 = jnp.where(r == c, 1.0 / HEAD_DIM, 0.0).astype(BF16)

    def qk_norm(t_ref, w_ref):
        t = t_ref[0].astype(F32)
        sq = t * t
        hi = sq.astype(BF16)
        lo = (sq - hi.astype(F32)).astype(BF16)
        ms = _dot(jnp.concatenate([hi, lo], axis=1), avg2)
        return t * lax.rsqrt(ms + NORM_EPS) * w_ref[...]

    qn = qk_norm(q_ref, qw_ref)
    kn = qk_norm(k_ref, kw_ref)
    qs = qn * (HEAD_DIM ** -0.5)
    qs_b = qs.astype(BF16)

    kmean = jnp.mean(kn.reshape(nb, MOBA_BLOCK, LANES), axis=1)
    kmx = jnp.concatenate([jnp.where(head0, kmean, 0.0), jnp.where(head0, 0.0, kmean)], axis=0)
    kmx_hi = kmx.astype(BF16)
    kmx_lo = (kmx - kmx_hi.astype(F32)).astype(BF16)
    st = _dot_nt(jnp.concatenate([kmx_hi, kmx_lo], axis=0), qs_b)
    st = st[:2 * nb] + st[2 * nb:]

    qblk = lax.broadcasted_iota(jnp.int32, (nb, seq), 1) // MOBA_BLOCK
    jrow = lax.broadcasted_iota(jnp.int32, (nb, seq), 0)
    past = jrow < qblk
    bias = []
    for a in range(2):
        sm = jnp.where(past, st[a * nb:(a + 1) * nb], NEG_INF)
        rank = jnp.zeros((nb, seq), jnp.int32)
        for jp in range(nb):
            other = sm[jp:jp + 1, :]
            ahead = (other > sm) | ((other == sm) & (jp < jrow))
            rank = rank + ahead.astype(jnp.int32)
        bias.append(jnp.where(past & (rank >= MOBA_TOPK), NEG_INF, 0.0).astype(F32))
    zpad = jnp.zeros((HEAD_DIM - nb, seq), F32)
    bias_t = jnp.concatenate([bias[1], zpad, bias[0], zpad], axis=0).T

    rblk = lax.broadcasted_iota(jnp.int32, (seq, LANES), 0) // MOBA_BLOCK
    l64 = lax.broadcasted_iota(jnp.int32, (seq, LANES), 1) % HEAD_DIM
    ind = (l64 == rblk).astype(F32)

    q_aug = (jnp.where(head0, qs, bias_t).astype(BF16), jnp.where(head0, bias_t, qs).astype(BF16))
    k_aug = (jnp.where(head0, kn, ind).astype(BF16), jnp.where(head0, ind, kn).astype(BF16))
    v = v_ref[0]
    one = jnp.ones((), BF16)
    v_aug = (jnp.where(head0, v, one), jnp.where(head0, one, v))

    tri = (lax.broadcasted_iota(jnp.int32, (MOBA_BLOCK, MOBA_BLOCK), 0)
           >= lax.broadcasted_iota(jnp.int32, (MOBA_BLOCK, MOBA_BLOCK), 1))
    for i in range(nb):
        lo, hi = i * MOBA_BLOCK, (i + 1) * MOBA_BLOCK
        outs = []
        for a in range(2):
            s = _dot_nt(q_aug[a][lo:hi], k_aug[a][:hi])
            own = jnp.where(tri, s[:, lo:], NEG_INF)
            s = own if i == 0 else jnp.concatenate([s[:, :lo], own], axis=1)
            m = jnp.max(s, axis=1, keepdims=True)
            p = jnp.exp(s - m).astype(BF16)
            o = _dot(p, v_aug[a][:hi])
            outs.append(o / pltpu.roll(o, HEAD_DIM, axis=1))
        o_ref[0, lo:hi, :] = jnp.where(head0, outs[0], outs[1]).astype(o_ref.dtype)


def _moba(proj3, qw2, kw2):
    b, seq, _ = proj3.shape
    nb = seq // MOBA_BLOCK
    npair = ATTN_WIDTH // LANES
    return pl.pallas_call(
        functools.partial(_moba_kernel, seq=seq, nb=nb),
        grid=(b, npair),
        in_specs=[
            pl.BlockSpec((1, seq, LANES), lambda i, j: (i, 0, j)),
            pl.BlockSpec((1, seq, LANES), lambda i, j: (i, 0, npair + j)),
            pl.BlockSpec((1, seq, LANES), lambda i, j: (i, 0, 2 * npair + j)),
            pl.BlockSpec((1, LANES), lambda i, j: (0, 0)),
            pl.BlockSpec((1, LANES), lambda i, j: (0, 0)),
        ],
        out_specs=pl.BlockSpec((1, seq, LANES), lambda i, j: (i, 0, j)),
        out_shape=jax.ShapeDtypeStruct((b, seq, ATTN_WIDTH), BF16),
        compiler_params=pltpu.CompilerParams(
            dimension_semantics=("parallel", "parallel"), vmem_limit_bytes=VMEM_LIMIT),
        name="moba",
    )(proj3, proj3, proj3, qw2, kw2)


def _ssd_kernel(x_ref, b_ref, c_ref, z_ref, dt_ref, cwx_ref, cwb_ref, cwc_ref, cbx_ref, cbb_ref, cbc_ref,
                dtb_ref, alog_ref, dskip_ref, nw_ref, o_ref, px_ref, pb_ref, pc_ref, st_ref):
    g = pl.program_id(1)
    ch = SSD_CHUNK

    @pl.when(pl.program_id(2) == 0)
    def _():
        px_ref[...] = jnp.zeros_like(px_ref)
        pb_ref[...] = jnp.zeros_like(pb_ref)
        pc_ref[...] = jnp.zeros_like(pc_ref)
        st_ref[...] = jnp.zeros_like(st_ref)

    def conv_silu(cur_ref, prev_ref, w_ref, bias_ref):
        cur = cur_ref[0].astype(F32)
        cat = jnp.concatenate([prev_ref[...], cur], axis=0)
        w = w_ref[...]
        out = cur * w[CONV_K - 1:CONV_K, :]
        for k in range(1, CONV_K):
            out = out + pltpu.roll(cat, k, axis=0)[SUBLANES:] * w[CONV_K - 1 - k:CONV_K - k, :]
        prev_ref[...] = cur[ch - SUBLANES:]
        out = out + bias_ref[...]
        return out * jax.nn.sigmoid(out)

    xs = conv_silu(x_ref, px_ref, cwx_ref, cbx_ref)
    bm = conv_silu(b_ref, pb_ref, cwb_ref, cbb_ref)
    cm = conv_silu(c_ref, pc_ref, cwc_ref, cbc_ref)

    ri = lax.broadcasted_iota(jnp.int32, (LANES, LANES), 0)
    ci = lax.broadcasted_iota(jnp.int32, (LANES, LANES), 1)
    sel_g = ((ri == g * HEADS_PER_GROUP + ci) & (ci < HEADS_PER_GROUP)).astype(BF16)
    dtr = _dot(dt_ref[0], sel_g) + dtb_ref[0]
    dtf = jnp.maximum(dtr, 0.0) + jnp.log1p(jnp.exp(-jnp.abs(dtr)))
    a = jnp.where(ci[0:1, :] < HEADS_PER_GROUP, dtf * (-jnp.exp(alog_ref[0])), 0.0)

    tril_b = ri >= ci
    tril = tril_b.astype(BF16)
    a_cum = _dot(jnp.concatenate([tril, tril, tril], axis=1), jnp.concatenate(_split3(a), axis=0))
    a_tot = a_cum[ch - 1:ch, :]

    er = lax.broadcasted_iota(jnp.int32, (LANES, GROUP_WIDTH), 0)
    ec = lax.broadcasted_iota(jnp.int32, (LANES, GROUP_WIDTH), 1) // SSD_HEAD_DIM
    e1 = (er == ec).astype(BF16)
    e3 = jnp.concatenate([e1, e1, e1], axis=0)
    vals = jnp.concatenate([dtf, jnp.exp(a_cum), jnp.exp(a_tot - a_cum)], axis=0)
    ex = _dot(jnp.concatenate(_split3(vals), axis=1), e3)
    dt_e, eac_e, dec_e = ex[:ch], ex[ch:2 * ch], ex[2 * ch:]
    etot_e = eac_e[ch - 1:ch, :]

    xdt = xs * dt_e
    xdt_b = xdt.astype(BF16)
    bm_b = bm.astype(BF16)
    cm_b = cm.astype(BF16)
    cb = _dot_nt(cm_b, bm_b)
    a_cum_t = a_cum.T
    lane_h0 = lax.broadcasted_iota(jnp.int32, (1, LANES), 1) < SSD_HEAD_DIM
    pieces = []
    for hp in range(HEADS_PER_GROUP // 2):
        xpair = xdt_b[:, hp * LANES:(hp + 1) * LANES]
        yd = []
        for hh in range(2):
            h = 2 * hp + hh
            seg = a_cum[:, h:h + 1] - a_cum_t[h:h + 1, :]
            lmat = jnp.exp(jnp.where(tril_b, seg, -jnp.inf))
            yd.append(_dot((cb * lmat).astype(BF16), xpair))
        pieces.append(jnp.where(lane_h0, yd[0], yd[1]))
    y_diag = jnp.concatenate(pieces, axis=1)

    s_prev = st_ref[...]
    y_off = _dot(cm_b, s_prev.astype(BF16)) * eac_e
    st_ref[...] = etot_e * s_prev + _dot(bm.T.astype(BF16), (xdt * dec_e).astype(BF16))

    y = y_diag + y_off + dskip_ref[0] * xs
    z = z_ref[0].astype(F32)
    gated = y * (z * jax.nn.sigmoid(z))
    ms = jnp.mean(gated * gated, axis=-1, keepdims=True)
    o_ref[0] = (gated * lax.rsqrt(ms + NORM_EPS) * nw_ref[0]).astype(o_ref.dtype)


def _ssd(proj3, conv_w, conv_b2, dtb_g, alog_g, dskip_e, nw_g):
    b, seq, _ = proj3.shape
    nc = seq // SSD_CHUNK
    gw, ch = GROUP_WIDTH, SSD_CHUNK
    xo, bo, co, zo = P_XBC // gw, (P_XBC + D_INNER) // LANES, (P_XBC + D_INNER) // LANES + N_GROUPS, P_Z // gw
    return pl.pallas_call(
        _ssd_kernel,
        grid=(b, N_GROUPS, nc),
        in_specs=[
            pl.BlockSpec((1, ch, gw), lambda i, g, c: (i, c, xo + g)),
            pl.BlockSpec((1, ch, LANES), lambda i, g, c: (i, c, bo + g)),
            pl.BlockSpec((1, ch, LANES), lambda i, g, c: (i, c, co + g)),
            pl.BlockSpec((1, ch, gw), lambda i, g, c: (i, c, zo + g)),
            pl.BlockSpec((1, ch, LANES), lambda i, g, c: (i, c, P_DT // LANES)),
            pl.BlockSpec((CONV_K, gw), lambda i, g, c: (0, g)),
            pl.BlockSpec((CONV_K, LANES), lambda i, g, c: (0, D_INNER // LANES + g)),
            pl.BlockSpec((CONV_K, LANES), lambda i, g, c: (0, D_INNER // LANES + N_GROUPS + g)),
            pl.BlockSpec((1, gw), lambda i, g, c: (0, g)),
            pl.BlockSpec((1, LANES), lambda i, g, c: (0, D_INNER // LANES + g)),
            pl.BlockSpec((1, LANES), lambda i, g, c: (0, D_INNER // LANES + N_GROUPS + g)),
            pl.BlockSpec((1, 1, LANES), lambda i, g, c: (g, 0, 0)),
            pl.BlockSpec((1, 1, LANES), lambda i, g, c: (g, 0, 0)),
            pl.BlockSpec((1, 1, gw), lambda i, g, c: (g, 0, 0)),
            pl.BlockSpec((1, 1, gw), lambda i, g, c: (g, 0, 0)),
        ],
        out_specs=pl.BlockSpec((1, ch, gw), lambda i, g, c: (i, c, g)),
        out_shape=jax.ShapeDtypeStruct((b, seq, D_INNER), BF16),
        scratch_shapes=[
            pltpu.VMEM((SUBLANES, gw), F32),
            pltpu.VMEM((SUBLANES, LANES), F32),
            pltpu.VMEM((SUBLANES, LANES), F32),
            pltpu.VMEM((D_STATE, gw), F32),
        ],
        compiler_params=pltpu.CompilerParams(
            dimension_semantics=("parallel", "parallel", "arbitrary"), vmem_limit_bytes=VMEM_LIMIT),
        name="ssd",
    )(proj3, proj3, proj3, proj3, proj3, conv_w, conv_w, conv_w, conv_b2, conv_b2, conv_b2,
      dtb_g, alog_g, dskip_e, nw_g)


def _merge_kernel(x_ref, attn_ref, ssd_ref, gl0_ref, gl1_ref, gb_ref, woa_ref, wos_ref, wout_ref,
                  ln2_ref, wr_ref, br_ref, x1_ref, h2_ref, ti_ref, tw_ref):
    ao = _dot(attn_ref[...], woa_ref[...])
    so = _dot(ssd_ref[...], wos_ref[...])
    g0 = jax.nn.sigmoid(gl0_ref[...].astype(F32) + gb_ref[0:1, :])
    g1 = jax.nn.sigmoid(gl1_ref[...].astype(F32) + gb_ref[1:2, :])
    mixed = (g0 * ao + g1 * so).astype(BF16)
    x1 = x_ref[...] + _dot(mixed, wout_ref[...])
    x1_ref[...] = x1
    ms = jnp.mean(x1 * x1, axis=-1, keepdims=True)
    h2 = x1 * lax.rsqrt(ms + NORM_EPS) * ln2_ref[...]
    h2_ref[...] = h2

    hh = h2.astype(BF16)
    hl = (h2 - hh.astype(F32)).astype(BF16)
    r = _dot(jnp.concatenate([hh, hl], axis=1), wr_ref[...])
    logits = r + pltpu.roll(r, LANES - N_EXPERTS, axis=1) + br_ref[...]
    lane = lax.broadcasted_iota(jnp.int32, logits.shape, 1)
    lane_f = lane.astype(F32)
    cur = jnp.where(lane < N_EXPERTS, logits, -jnp.inf)
    vals, idxs = [], []
    for _ in range(TOP_K):
        m = jnp.max(cur, axis=1, keepdims=True)
        idx = jnp.min(jnp.where(cur == m, lane_f, float(LANES)), axis=1,
                      keepdims=True).astype(jnp.int32)
        vals.append(m)
        idxs.append(idx)
        cur = jnp.where(lane == idx, -jnp.inf, cur)
    es = [jnp.exp(v - vals[0]) for v in vals]
    den = es[0] + es[1] + es[2] + es[3]
    ti = jnp.zeros(logits.shape, jnp.int32)
    tw = jnp.zeros(logits.shape, F32)
    for k in range(TOP_K):
        ti = jnp.where(lane == k, idxs[k], ti)
        tw = jnp.where(lane == k, es[k] / den, tw)
    ti_ref[...] = ti
    tw_ref[...] = tw


def _merge(x2, attn2, ssd2, proj, gate_b, woa, wos, wout, ln2, wr, br):
    t = x2.shape[0]
    tm = min(512, t)
    const = lambda i: (0, 0)
    gcol = P_GATE // D_MODEL
    return pl.pallas_call(
        _merge_kernel,
        grid=(t // tm,),
        in_specs=[
            pl.BlockSpec((tm, D_MODEL), lambda i: (i, 0)),
            pl.BlockSpec((tm, ATTN_WIDTH), lambda i: (i, 0)),
            pl.BlockSpec((tm, D_INNER), lambda i: (i, 0)),
            pl.BlockSpec((tm, D_MODEL), lambda i: (i, gcol)),
            pl.BlockSpec((tm, D_MODEL), lambda i: (i, gcol + 1)),
            pl.BlockSpec((2, D_MODEL), const),
            pl.BlockSpec((ATTN_WIDTH, D_MODEL), const),
            pl.BlockSpec((D_INNER, D_MODEL), const),
            pl.BlockSpec((D_MODEL, D_MODEL), const),
            pl.BlockSpec((1, D_MODEL), const),
            pl.BlockSpec((2 * D_MODEL, LANES), const),
            pl.BlockSpec((1, LANES), const),
        ],
        out_specs=[
            pl.BlockSpec((tm, D_MODEL), lambda i: (i, 0)),
            pl.BlockSpec((tm, D_MODEL), lambda i: (i, 0)),
            pl.BlockSpec((tm, LANES), lambda i: (i, 0)),
            pl.BlockSpec((tm, LANES), lambda i: (i, 0)),
        ],
        out_shape=[
            jax.ShapeDtypeStruct((t, D_MODEL), F32),
            jax.ShapeDtypeStruct((t, D_MODEL), F32),
            jax.ShapeDtypeStruct((t, LANES), jnp.int32),
            jax.ShapeDtypeStruct((t, LANES), F32),
        ],
        compiler_params=pltpu.CompilerParams(
            dimension_semantics=("parallel",), vmem_limit_bytes=VMEM_LIMIT),
        name="merge",
    )(x2, attn2, ssd2, proj, proj, gate_b, woa, wos, wout, ln2, wr, br)


def _expert_kernel(be_ref, tok_ref, h_hbm, wg_ref, wu_ref, bg_ref, bu_ref, wd_ref, bd_ref, y_ref, xbuf, sem):
    i = pl.program_id(0)
    nblk = pl.num_programs(0)
    slot = lax.rem(i, 2)

    def row_copy(blk, r, s):
        t = tok_ref[blk * MOE_BLOCK + r]
        return pltpu.make_async_copy(h_hbm.at[pl.ds(t, 1), :], xbuf.at[s, pl.ds(r, 1), :], sem.at[s])

    def issue(blk, s):
        def body(r, carry):
            row_copy(blk, r, s).start()
            return carry
        lax.fori_loop(0, MOE_BLOCK, body, 0, unroll=8)

    def wait(blk, s):
        def body(r, carry):
            row_copy(blk, r, s).wait()
            return carry
        lax.fori_loop(0, MOE_BLOCK, body, 0, unroll=8)

    @pl.when(i == 0)
    def _():
        issue(0, 0)

    @pl.when(i + 1 < nblk)
    def _():
        issue(i + 1, 1 - slot)

    wait(i, slot)
    x = xbuf[slot].astype(BF16)
    g = jnp.minimum(_dot(x, wg_ref[0]) + bg_ref[0], SWIGLU_LIMIT)
    u = jnp.clip(_dot(x, wu_ref[0]) + bu_ref[0], -SWIGLU_LIMIT, SWIGLU_LIMIT)
    act = (u + 1.0) * (g * jax.nn.sigmoid(SWIGLU_ALPHA * g))
    y_ref[...] = _dot(act.astype(BF16), wd_ref[0]) + bd_ref[0]


def _experts(block_expert, buf_tok, h2, wg, wu, bg, bu, wd, bd):
    n_rows = buf_tok.shape[0]
    n_blocks = n_rows // MOE_BLOCK
    wmap = lambda i, be, tok: (be[i], 0, 0)
    return pl.pallas_call(
        _expert_kernel,
        grid_spec=pltpu.PrefetchScalarGridSpec(
            num_scalar_prefetch=2,
            grid=(n_blocks,),
            in_specs=[
                pl.BlockSpec(memory_space=pl.ANY),
                pl.BlockSpec((1, D_MODEL, D_FF), wmap),
                pl.BlockSpec((1, D_MODEL, D_FF), wmap),
                pl.BlockSpec((1, 1, D_FF), wmap),
                pl.BlockSpec((1, 1, D_FF), wmap),
                pl.BlockSpec((1, D_FF, D_MODEL), wmap),
                pl.BlockSpec((1, 1, D_MODEL), wmap),
            ],
            out_specs=pl.BlockSpec((MOE_BLOCK, D_MODEL), lambda i, be, tok: (i, 0)),
            scratch_shapes=[
                pltpu.VMEM((2, MOE_BLOCK, D_MODEL), F32),
                pltpu.SemaphoreType.DMA((2,)),
            ],
        ),
        out_shape=jax.ShapeDtypeStruct((n_rows, D_MODEL), F32),
        compiler_params=pltpu.CompilerParams(
            dimension_semantics=("arbitrary",), vmem_limit_bytes=VMEM_LIMIT),
        name="experts",
    )(block_expert, buf_tok, h2, wg, wu, bg, bu, wd, bd)


COMBINE_TM = 128


def _combine_kernel(dest_ref, x1_ref, tw_ref, y_hbm, o_ref, buf, sem):
    i = pl.program_id(0)
    nt = pl.num_programs(0)
    slot = lax.rem(i, 2)
    tm = COMBINE_TM

    def row_copy(tile, r, k, s):
        d = dest_ref[(tile * tm + r) * TOP_K + k]
        return pltpu.make_async_copy(y_hbm.at[pl.ds(d, 1), :], buf.at[s, k, pl.ds(r, 1), :], sem.at[s])

    def issue(tile, s):
        def body(r, carry):
            for k in range(TOP_K):
                row_copy(tile, r, k, s).start()
            return carry
        lax.fori_loop(0, tm, body, 0, unroll=2)

    def wait(tile, s):
        def body(r, carry):
            for k in range(TOP_K):
                row_copy(tile, r, k, s).wait()
            return carry
        lax.fori_loop(0, tm, body, 0, unroll=2)

    @pl.when(i == 0)
    def _():
        issue(0, 0)

    @pl.when(i + 1 < nt)
    def _():
        issue(i + 1, 1 - slot)

    wait(i, slot)
    tw = tw_ref[...]
    acc = x1_ref[...]
    for k in range(TOP_K):
        acc = acc + buf[slot, k] * tw[:, k:k + 1]
    o_ref[...] = acc


def _combine(dest_flat, x1, tw, yb):
    t = x1.shape[0]
    tm = COMBINE_TM
    return pl.pallas_call(
        _combine_kernel,
        grid_spec=pltpu.PrefetchScalarGridSpec(
            num_scalar_prefetch=1,
            grid=(t // tm,),
            in_specs=[
                pl.BlockSpec((tm, D_MODEL), lambda i, d: (i, 0)),
                pl.BlockSpec((tm, LANES), lambda i, d: (i, 0)),
                pl.BlockSpec(memory_space=pl.ANY),
            ],
            out_specs=pl.BlockSpec((tm, D_MODEL), lambda i, d: (i, 0)),
            scratch_shapes=[
                pltpu.VMEM((2, TOP_K, tm, D_MODEL), F32),
                pltpu.SemaphoreType.DMA((2,)),
            ],
        ),
        out_shape=jax.ShapeDtypeStruct((t, D_MODEL), F32),
        compiler_params=pltpu.CompilerParams(
            dimension_semantics=("arbitrary",), vmem_limit_bytes=VMEM_LIMIT),
        name="combine",
    )(dest_flat, x1, tw, yb)


def _routing_tables(top_idx, t):
    multi_hot = jnp.sum(jax.nn.one_hot(top_idx, N_EXPERTS, dtype=jnp.int32), axis=1)
    before = jnp.cumsum(multi_hot, axis=0) - multi_hot
    pos = jnp.take_along_axis(before, top_idx, axis=1)
    counts = jnp.sum(multi_hot, axis=0)
    padded = ((counts + MOE_BLOCK - 1) // MOE_BLOCK) * MOE_BLOCK
    cum_padded = jnp.cumsum(padded)
    pstart = cum_padded - padded
    dest = (pstart[top_idx] + pos).astype(jnp.int32)
    n_rows = t * TOP_K + N_EXPERTS * MOE_BLOCK
    n_blocks = n_rows // MOE_BLOCK
    tok = jnp.repeat(jnp.arange(t, dtype=jnp.int32), TOP_K)
    buf_tok = jnp.zeros((n_rows,), jnp.int32).at[dest.reshape(-1)].set(tok)
    block_expert = jnp.minimum(
        jnp.searchsorted(cum_padded, jnp.arange(n_blocks) * MOE_BLOCK, side='right'),
        N_EXPERTS - 1).astype(jnp.int32)
    return dest, buf_tok, block_expert


def kernel(x, ln1_w, w_in, gate_b, q_norm_w, k_norm_w, conv_w, conv_b, dt_bias, a_log, d_skip, ssd_norm_w,
           w_o_attn, w_o_ssd, w_out, ln2_w, w_router, b_router, w_gate_up, b_gate_up, w_down, b_down):
    b, seq, d = x.shape
    t = b * seq
    assert d == D_MODEL and seq % MOBA_BLOCK == 0 and seq % SSD_CHUNK == 0
    assert ln1_w.shape[0] == 1, "single layer"
    x2 = x.reshape(t, d)

    wi = w_in[0]
    col_dt = 3 * ATTN_WIDTH + D_INNER + D_XBC
    w_r = jnp.concatenate(
        [wi[:, :col_dt], wi[:, col_dt + SSD_HEADS:], wi[:, col_dt:col_dt + SSD_HEADS],
         jnp.zeros((d, LANES - SSD_HEADS), wi.dtype)], axis=1).astype(BF16)
    qw2 = jnp.tile(q_norm_w[0], 2)[None, :]
    kw2 = jnp.tile(k_norm_w[0], 2)[None, :]
    pad_g = lambda v: jnp.pad(v.reshape(N_GROUPS, 1, HEADS_PER_GROUP),
                              ((0, 0), (0, 0), (0, LANES - HEADS_PER_GROUP)))
    dtb_g = pad_g(dt_bias[0])
    alog_g = pad_g(a_log[0])
    dskip_e = jnp.repeat(d_skip[0], SSD_HEAD_DIM).reshape(N_GROUPS, 1, GROUP_WIDTH)
    nw_g = ssd_norm_w[0].reshape(N_GROUPS, 1, GROUP_WIDTH)
    wr_hi = w_router[0].astype(BF16)
    wr_lo = (w_router[0] - wr_hi.astype(F32)).astype(BF16)
    zr = jnp.zeros((d, N_EXPERTS), BF16)
    wr_p = jnp.concatenate([jnp.concatenate([wr_hi, wr_lo, zr, zr], axis=1),
                            jnp.concatenate([wr_hi, zr, zr, zr], axis=1)], axis=0)
    br_p = jnp.pad(b_router[0], (0, LANES - N_EXPERTS))[None, :]
    wgu = w_gate_up[0]
    wg = wgu[:, :, 0::2].astype(BF16)
    wu = wgu[:, :, 1::2].astype(BF16)
    bg = b_gate_up[0][:, None, 0::2]
    bu = b_gate_up[0][:, None, 1::2]
    wd = w_down[0].astype(BF16)
    bd = b_down[0][:, None, :]

    proj = _inproj(x2, ln1_w, w_r)
    proj3 = proj.reshape(b, seq, NP)
    attn = _moba(proj3, qw2, kw2)
    ssd = _ssd(proj3, conv_w[0], conv_b, dtb_g, alog_g, dskip_e, nw_g)
    x1, h2, ti, tw = _merge(x2, attn.reshape(t, ATTN_WIDTH), ssd.reshape(t, D_INNER), proj, gate_b[0],
                            w_o_attn[0].astype(BF16), w_o_ssd[0].astype(BF16), w_out[0].astype(BF16),
                            ln2_w, wr_p, br_p)

    dest, buf_tok, block_expert = _routing_tables(ti[:, :TOP_K], t)
    yb = _experts(block_expert, buf_tok, h2, wg, wu, bg, bu, wd, bd)
    out = _combine(dest.reshape(-1), x1, tw, yb)
    return out.reshape(b, seq, d)
```

```python
import functools

import jax
import jax.numpy as jnp
from jax import lax
from jax.experimental import pallas as pl
from jax.experimental.pallas import tpu as pltpu

F32 = jnp.float32
BF16 = jnp.bfloat16

D_MODEL = 1024
N_HEADS = 16
HEAD_DIM = 64
ATTN_WIDTH = N_HEADS * HEAD_DIM
MOBA_BLOCK = 256
MOBA_TOPK = 3
D_INNER = 2048
SSD_HEAD_DIM = 64
SSD_HEADS = D_INNER // SSD_HEAD_DIM
N_GROUPS = 4
HEADS_PER_GROUP = SSD_HEADS // N_GROUPS
GROUP_WIDTH = HEADS_PER_GROUP * SSD_HEAD_DIM
D_STATE = 128
CONV_K = 4
SSD_CHUNK = 128
D_XBC = D_INNER + 2 * N_GROUPS * D_STATE
N_EXPERTS = 32
TOP_K = 4
D_FF = D_MODEL
SWIGLU_LIMIT = 7.0
SWIGLU_ALPHA = 1.702
MOE_BLOCK = 256
NORM_EPS = 1e-6
NEG_INF = -1e30

LANES = 128
SUBLANES = 8

P_Q = 0
P_Z = 3 * ATTN_WIDTH
P_XBC = P_Z + D_INNER
P_GATE = P_XBC + D_XBC
P_DT = P_GATE + 2 * D_MODEL
NP = P_DT + LANES
PROJ_TN = 1152
VMEM_LIMIT = 56 * 1024 * 1024


def _split3(v):
    hi = v.astype(BF16)
    r1 = v - hi.astype(F32)
    mid = r1.astype(BF16)
    lo = (r1 - mid.astype(F32)).astype(BF16)
    return hi, mid, lo


def _dot(a, b):
    return jnp.dot(a, b, preferred_element_type=F32)


def _dot_nt(a, b):
    return lax.dot_general(a, b, (((1,), (1,)), ((), ())), preferred_element_type=F32)


ROW_TILE = D_MODEL // LANES


def _store_row_tiles(ref, val):
    rows = val.shape[0]
    for c in range(ROW_TILE):
        ref[pl.ds(c, rows, stride=ROW_TILE), :] = val[:, c * LANES:(c + 1) * LANES]


def _load_row_tiles(ref, rows):
    return jnp.concatenate([ref[pl.ds(c, rows, stride=ROW_TILE), :] for c in range(ROW_TILE)], axis=1)


def _inproj_kernel(x_ref, lnw_ref, w_ref, o_ref, h_ref):
    @pl.when(pl.program_id(1) == 0)
    def _():
        x = x_ref[...]
        ms = jnp.mean(x * x, axis=-1, keepdims=True)
        h_ref[...] = (x * lax.rsqrt(ms + NORM_EPS) * lnw_ref[...]).astype(BF16)

    o_ref[...] = _dot(h_ref[...], w_ref[...]).astype(o_ref.dtype)


def _inproj(x2, lnw, w_r):
    t = x2.shape[0]
    tm = min(1024, t)
    return pl.pallas_call(
        _inproj_kernel,
        grid=(t // tm, NP // PROJ_TN),
        in_specs=[
            pl.BlockSpec((tm, D_MODEL), lambda i, j: (i, 0)),
            pl.BlockSpec((1, D_MODEL), lambda i, j: (0, 0)),
            pl.BlockSpec((D_MODEL, PROJ_TN), lambda i, j: (0, j)),
        ],
        out_specs=pl.BlockSpec((tm, PROJ_TN), lambda i, j: (i, j)),
        out_shape=jax.ShapeDtypeStruct((t, NP), BF16),
        scratch_shapes=[pltpu.VMEM((tm, D_MODEL), BF16)],
        compiler_params=pltpu.CompilerParams(
            dimension_semantics=("parallel", "arbitrary"), vmem_limit_bytes=VMEM_LIMIT),
        name="inproj",
    )(x2, lnw, w_r)


def _moba_kernel(q_ref, k_ref, v_ref, qw_ref, kw_ref, o_ref, *, seq, nb):
    lane = lax.broadcasted_iota(jnp.int32, (1, LANES), 1)
    head0 = lane < HEAD_DIM
    r = lax.broadcasted_iota(jnp.int32, (2 * LANES, LANES), 0) % LANES // HEAD_DIM
    c = lax.broadcasted_iota(jnp.int32, (2 * LANES, LANES), 1) // HEAD_DIM
    avg2 = jnp.where(r == c, 1.0 / HEAD_DIM, 0.0).astype(BF16)

    def qk_norm(t_ref, w_ref):
        t = t_ref[0].astype(F32)
        sq = t * t
        hi = sq.astype(BF16)
        lo = (sq - hi.astype(F32)).astype(BF16)
        ms = _dot(jnp.concatenate([hi, lo], axis=1), avg2)
        return t * lax.rsqrt(ms + NORM_EPS) * w_ref[...]

    qn = qk_norm(q_ref, qw_ref)
    kn = qk_norm(k_ref, kw_ref)
    qs = qn * (HEAD_DIM ** -0.5)
    qs_b = qs.astype(BF16)

    kmean = jnp.mean(kn.reshape(nb, MOBA_BLOCK, LANES), axis=1)
    kmx = jnp.concatenate([jnp.where(head0, kmean, 0.0), jnp.where(head0, 0.0, kmean)], axis=0)
    kmx_hi = kmx.astype(BF16)
    kmx_lo = (kmx - kmx_hi.astype(F32)).astype(BF16)
    st = _dot_nt(jnp.concatenate([kmx_hi, kmx_lo], axis=0), qs_b)
    st = st[:2 * nb] + st[2 * nb:]

    qblk = lax.broadcasted_iota(jnp.int32, (nb, seq), 1) // MOBA_BLOCK
    jrow = lax.broadcasted_iota(jnp.int32, (nb, seq), 0)
    past = jrow < qblk
    bias = []
    for a in range(2):
        sm = jnp.where(past, st[a * nb:(a + 1) * nb], NEG_INF)
        rank = jnp.zeros((nb, seq), jnp.int32)
        for jp in range(nb):
            other = sm[jp:jp + 1, :]
            ahead = (other > sm) | ((other == sm) & (jp < jrow))
            rank = rank + ahead.astype(jnp.int32)
        bias.append(jnp.where(past & (rank >= MOBA_TOPK), NEG_INF, 0.0).astype(F32))
    zpad = jnp.zeros((HEAD_DIM - nb, seq), F32)
    bias_t = jnp.concatenate([bias[1], zpad, bias[0], zpad], axis=0).T

    rblk = lax.broadcasted_iota(jnp.int32, (seq, LANES), 0) // MOBA_BLOCK
    l64 = lax.broadcasted_iota(jnp.int32, (seq, LANES), 1) % HEAD_DIM
    ind = (l64 == rblk).astype(F32)

    q_aug = (jnp.where(head0, qs, bias_t).astype(BF16), jnp.where(head0, bias_t, qs).astype(BF16))
    k_aug = (jnp.where(head0, kn, ind).astype(BF16), jnp.where(head0, ind, kn).astype(BF16))
    v = v_ref[0]
    one = jnp.ones((), BF16)
    v_aug = (jnp.where(head0, v, one), jnp.where(head0, one, v))

    tri = (lax.broadcasted_iota(jnp.int32, (MOBA_BLOCK, MOBA_BLOCK), 0)
           >= lax.broadcasted_iota(jnp.int32, (MOBA_BLOCK, MOBA_BLOCK), 1))
    for i in range(nb):
        lo, hi = i * MOBA_BLOCK, (i + 1) * MOBA_BLOCK
        outs = []
        for a in range(2):
            s = _dot_nt(q_aug[a][lo:hi], k_aug[a][:hi])
            own = jnp.where(tri, s[:, lo:], NEG_INF)
            s = own if i == 0 else jnp.concatenate([s[:, :lo], own], axis=1)
            m = jnp.max(s, axis=1, keepdims=True)
            p = jnp.exp(s - m).astype(BF16)
            o = _dot(p, v_aug[a][:hi])
            outs.append(o / pltpu.roll(o, HEAD_DIM, axis=1))
        o_ref[0, lo:hi, :] = jnp.where(head0, outs[0], outs[1]).astype(o_ref.dtype)


def _moba(proj3, qw2, kw2):
    b, seq, _ = proj3.shape
    nb = seq // MOBA_BLOCK
    npair = ATTN_WIDTH // LANES
    return pl.pallas_call(
        functools.partial(_moba_kernel, seq=seq, nb=nb),
        grid=(b, npair),
        in_specs=[
            pl.BlockSpec((1, seq, LANES), lambda i, j: (i, 0, j)),
            pl.BlockSpec((1, seq, LANES), lambda i, j: (i, 0, npair + j)),
            pl.BlockSpec((1, seq, LANES), lambda i, j: (i, 0, 2 * npair + j)),
            pl.BlockSpec((1, LANES), lambda i, j: (0, 0)),
            pl.BlockSpec((1, LANES), lambda i, j: (0, 0)),
        ],
        out_specs=pl.BlockSpec((1, seq, LANES), lambda i, j: (i, 0, j)),
        out_shape=jax.ShapeDtypeStruct((b, seq, ATTN_WIDTH), BF16),
        compiler_params=pltpu.CompilerParams(
            dimension_semantics=("parallel", "parallel"), vmem_limit_bytes=VMEM_LIMIT),
        name="moba",
    )(proj3, proj3, proj3, qw2, kw2)


def _ssd_kernel(x_ref, b_ref, c_ref, z_ref, dt_ref, cwx_ref, cwb_ref, cwc_ref, cbx_ref, cbb_ref, cbc_ref,
                dtb_ref, alog_ref, dskip_ref, nw_ref, o_ref, px_ref, pb_ref, pc_ref, st_ref):
    g = pl.program_id(1)
    ch = SSD_CHUNK

    @pl.when(pl.program_id(2) == 0)
    def _():
        px_ref[...] = jnp.zeros_like(px_ref)
        pb_ref[...] = jnp.zeros_like(pb_ref)
        pc_ref[...] = jnp.zeros_like(pc_ref)
        st_ref[...] = jnp.zeros_like(st_ref)

    def conv_silu(cur_ref, prev_ref, w_ref, bias_ref):
        cur = cur_ref[0].astype(F32)
        cat = jnp.concatenate([prev_ref[...], cur], axis=0)
        w = w_ref[...]
        out = cur * w[CONV_K - 1:CONV_K, :]
        for k in range(1, CONV_K):
            out = out + pltpu.roll(cat, k, axis=0)[SUBLANES:] * w[CONV_K - 1 - k:CONV_K - k, :]
        prev_ref[...] = cur[ch - SUBLANES:]
        out = out + bias_ref[...]
        return out * jax.nn.sigmoid(out)

    xs = conv_silu(x_ref, px_ref, cwx_ref, cbx_ref)
    bm = conv_silu(b_ref, pb_ref, cwb_ref, cbb_ref)
    cm = conv_silu(c_ref, pc_ref, cwc_ref, cbc_ref)

    ri = lax.broadcasted_iota(jnp.int32, (LANES, LANES), 0)
    ci = lax.broadcasted_iota(jnp.int32, (LANES, LANES), 1)
    sel_g = ((ri == g * HEADS_PER_GROUP + ci) & (ci < HEADS_PER_GROUP)).astype(BF16)
    dtr = _dot(dt_ref[0], sel_g) + dtb_ref[0]
    dtf = jnp.maximum(dtr, 0.0) + jnp.log1p(jnp.exp(-jnp.abs(dtr)))
    a = jnp.where(ci[0:1, :] < HEADS_PER_GROUP, dtf * (-jnp.exp(alog_ref[0])), 0.0)

    tril_b = ri >= ci
    tril = tril_b.astype(BF16)
    a_cum = _dot(jnp.concatenate([tril, tril, tril], axis=1), jnp.concatenate(_split3(a), axis=0))
    a_tot = a_cum[ch - 1:ch, :]

    er = lax.broadcasted_iota(jnp.int32, (LANES, GROUP_WIDTH), 0)
    ec = lax.broadcasted_iota(jnp.int32, (LANES, GROUP_WIDTH), 1) // SSD_HEAD_DIM
    e1 = (er == ec).astype(BF16)
    e3 = jnp.concatenate([e1, e1, e1], axis=0)
    vals = jnp.concatenate([dtf, jnp.exp(a_cum), jnp.exp(a_tot - a_cum)], axis=0)
    ex = _dot(jnp.concatenate(_split3(vals), axis=1), e3)
    dt_e, eac_e, dec_e = ex[:ch], ex[ch:2 * ch], ex[2 * ch:]
    etot_e = eac_e[ch - 1:ch, :]

    xdt = xs * dt_e
    xdt_b = xdt.astype(BF16)
    bm_b = bm.astype(BF16)
    cm_b = cm.astype(BF16)
    cb = _dot_nt(cm_b, bm_b)
    a_cum_t = a_cum.T
    lane_h0 = lax.broadcasted_iota(jnp.int32, (1, LANES), 1) < SSD_HEAD_DIM
    pieces = []
    for hp in range(HEADS_PER_GROUP // 2):
        xpair = xdt_b[:, hp * LANES:(hp + 1) * LANES]
        yd = []
        for hh in range(2):
            h = 2 * hp + hh
            seg = a_cum[:, h:h + 1] - a_cum_t[h:h + 1, :]
            lmat = jnp.exp(jnp.where(tril_b, seg, -jnp.inf))
            yd.append(_dot((cb * lmat).astype(BF16), xpair))
        pieces.append(jnp.where(lane_h0, yd[0], yd[1]))
    y_diag = jnp.concatenate(pieces, axis=1)

    s_prev = st_ref[...]
    y_off = _dot(cm_b, s_prev.astype(BF16)) * eac_e
    st_ref[...] = etot_e * s_prev + _dot(bm.T.astype(BF16), (xdt * dec_e).astype(BF16))

    y = y_diag + y_off + dskip_ref[0] * xs
    z = z_ref[0].astype(F32)
    gated = y * (z * jax.nn.sigmoid(z))
    ms = jnp.mean(gated * gated, axis=-1, keepdims=True)
    o_ref[0] = (gated * lax.rsqrt(ms + NORM_EPS) * nw_ref[0]).astype(o_ref.dtype)


def _ssd(proj3, conv_w, conv_b2, dtb_g, alog_g, dskip_e, nw_g):
    b, seq, _ = proj3.shape
    nc = seq // SSD_CHUNK
    gw, ch = GROUP_WIDTH, SSD_CHUNK
    xo, bo, co, zo = P_XBC // gw, (P_XBC + D_INNER) // LANES, (P_XBC + D_INNER) // LANES + N_GROUPS, P_Z // gw
    return pl.pallas_call(
        _ssd_kernel,
        grid=(b, N_GROUPS, nc),
        in_specs=[
            pl.BlockSpec((1, ch, gw), lambda i, g, c: (i, c, xo + g)),
            pl.BlockSpec((1, ch, LANES), lambda i, g, c: (i, c, bo + g)),
            pl.BlockSpec((1, ch, LANES), lambda i, g, c: (i, c, co + g)),
            pl.BlockSpec((1, ch, gw), lambda i, g, c: (i, c, zo + g)),
            pl.BlockSpec((1, ch, LANES), lambda i, g, c: (i, c, P_DT // LANES)),
            pl.BlockSpec((CONV_K, gw), lambda i, g, c: (0, g)),
            pl.BlockSpec((CONV_K, LANES), lambda i, g, c: (0, D_INNER // LANES + g)),
            pl.BlockSpec((CONV_K, LANES), lambda i, g, c: (0, D_INNER // LANES + N_GROUPS + g)),
            pl.BlockSpec((1, gw), lambda i, g, c: (0, g)),
            pl.BlockSpec((1, LANES), lambda i, g, c: (0, D_INNER // LANES + g)),
            pl.BlockSpec((1, LANES), lambda i, g, c: (0, D_INNER // LANES + N_GROUPS + g)),
            pl.BlockSpec((1, 1, LANES), lambda i, g, c: (g, 0, 0)),
            pl.BlockSpec((1, 1, LANES), lambda i, g, c: (g, 0, 0)),
            pl.BlockSpec((1, 1, gw), lambda i, g, c: (g, 0, 0)),
            pl.BlockSpec((1, 1, gw), lambda i, g, c: (g, 0, 0)),
        ],
        out_specs=pl.BlockSpec((1, ch, gw), lambda i, g, c: (i, c, g)),
        out_shape=jax.ShapeDtypeStruct((b, seq, D_INNER), BF16),
        scratch_shapes=[
            pltpu.VMEM((SUBLANES, gw), F32),
            pltpu.VMEM((SUBLANES, LANES), F32),
            pltpu.VMEM((SUBLANES, LANES), F32),
            pltpu.VMEM((D_STATE, gw), F32),
        ],
        compiler_params=pltpu.CompilerParams(
            dimension_semantics=("parallel", "parallel", "arbitrary"), vmem_limit_bytes=VMEM_LIMIT),
        name="ssd",
    )(proj3, proj3, proj3, proj3, proj3, conv_w, conv_w, conv_w, conv_b2, conv_b2, conv_b2,
      dtb_g, alog_g, dskip_e, nw_g)


def _merge_kernel(x_ref, attn_ref, ssd_ref, gl0_ref, gl1_ref, gb_ref, woa_ref, wos_ref, wout_ref,
                  ln2_ref, wr_ref, br_ref, x1_ref, h2_ref, ti_ref, tw_ref):
    ao = _dot(attn_ref[...], woa_ref[...])
    so = _dot(ssd_ref[...], wos_ref[...])
    g0 = jax.nn.sigmoid(gl0_ref[...].astype(F32) + gb_ref[0:1, :])
    g1 = jax.nn.sigmoid(gl1_ref[...].astype(F32) + gb_ref[1:2, :])
    mixed = (g0 * ao + g1 * so).astype(BF16)
    x1 = x_ref[...] + _dot(mixed, wout_ref[...])
    x1_ref[...] = x1
    ms = jnp.mean(x1 * x1, axis=-1, keepdims=True)
    h2 = x1 * lax.rsqrt(ms + NORM_EPS) * ln2_ref[...]
    _store_row_tiles(h2_ref, h2)

    hh = h2.astype(BF16)
    hl = (h2 - hh.astype(F32)).astype(BF16)
    r = _dot(jnp.concatenate([hh, hl], axis=1), wr_ref[...])
    logits = r + pltpu.roll(r, LANES - N_EXPERTS, axis=1) + br_ref[...]
    lane = lax.broadcasted_iota(jnp.int32, logits.shape, 1)
    lane_f = lane.astype(F32)
    cur = jnp.where(lane < N_EXPERTS, logits, -jnp.inf)
    vals, idxs = [], []
    for _ in range(TOP_K):
        m = jnp.max(cur, axis=1, keepdims=True)
        idx = jnp.min(jnp.where(cur == m, lane_f, float(LANES)), axis=1,
                      keepdims=True).astype(jnp.int32)
        vals.append(m)
        idxs.append(idx)
        cur = jnp.where(lane == idx, -jnp.inf, cur)
    es = [jnp.exp(v - vals[0]) for v in vals]
    den = es[0] + es[1] + es[2] + es[3]
    ti = jnp.zeros(logits.shape, jnp.int32)
    tw = jnp.zeros(logits.shape, F32)
    for k in range(TOP_K):
        ti = jnp.where(lane == k, idxs[k], ti)
        tw = jnp.where(lane == k, es[k] / den, tw)
    ti_ref[...] = ti
    tw_ref[...] = tw


def _merge(x2, attn2, ssd2, proj, gate_b, woa, wos, wout, ln2, wr, br):
    t = x2.shape[0]
    tm = min(512, t)
    const = lambda i: (0, 0)
    gcol = P_GATE // D_MODEL
    return pl.pallas_call(
        _merge_kernel,
        grid=(t // tm,),
        in_specs=[
            pl.BlockSpec((tm, D_MODEL), lambda i: (i, 0)),
            pl.BlockSpec((tm, ATTN_WIDTH), lambda i: (i, 0)),
            pl.BlockSpec((tm, D_INNER), lambda i: (i, 0)),
            pl.BlockSpec((tm, D_MODEL), lambda i: (i, gcol)),
            pl.BlockSpec((tm, D_MODEL), lambda i: (i, gcol + 1)),
            pl.BlockSpec((2, D_MODEL), const),
            pl.BlockSpec((ATTN_WIDTH, D_MODEL), const),
            pl.BlockSpec((D_INNER, D_MODEL), const),
            pl.BlockSpec((D_MODEL, D_MODEL), const),
            pl.BlockSpec((1, D_MODEL), const),
            pl.BlockSpec((2 * D_MODEL, LANES), const),
            pl.BlockSpec((1, LANES), const),
        ],
        out_specs=[
            pl.BlockSpec((tm, D_MODEL), lambda i: (i, 0)),
            pl.BlockSpec((tm * ROW_TILE, LANES), lambda i: (i, 0)),
            pl.BlockSpec((tm, LANES), lambda i: (i, 0)),
            pl.BlockSpec((tm, LANES), lambda i: (i, 0)),
        ],
        out_shape=[
            jax.ShapeDtypeStruct((t, D_MODEL), F32),
            jax.ShapeDtypeStruct((t * ROW_TILE, LANES), F32),
            jax.ShapeDtypeStruct((t, LANES), jnp.int32),
            jax.ShapeDtypeStruct((t, LANES), F32),
        ],
        compiler_params=pltpu.CompilerParams(
            dimension_semantics=("parallel",), vmem_limit_bytes=VMEM_LIMIT),
        name="merge",
    )(x2, attn2, ssd2, proj, proj, gate_b, woa, wos, wout, ln2, wr, br)


GU_PAIR = 2 * LANES
WPREP_TN = 512


def _wprep_kernel(wgu_ref, wd_ref, ogu_ref, od_ref):
    r = lax.broadcasted_iota(jnp.int32, (GU_PAIR, GU_PAIR), 0)
    c = lax.broadcasted_iota(jnp.int32, (GU_PAIR, GU_PAIR), 1)
    src = jnp.where(c < LANES, 2 * c, 2 * (c - LANES) + 1)
    perm = (r == src).astype(BF16)
    w = wgu_ref[0].astype(BF16)
    for k in range(WPREP_TN // GU_PAIR):
        sl = slice(k * GU_PAIR, (k + 1) * GU_PAIR)
        ogu_ref[0, :, sl] = _dot(w[:, sl], perm).astype(BF16)
    od_ref[0] = wd_ref[0].astype(BF16)


def _wprep(w_gate_up, w_down):
    e = w_gate_up.shape[0]
    nj = 2 * D_FF // WPREP_TN
    rows_d = D_FF // nj
    return pl.pallas_call(
        _wprep_kernel,
        grid=(e, nj),
        in_specs=[
            pl.BlockSpec((1, D_MODEL, WPREP_TN), lambda i, j: (i, 0, j)),
            pl.BlockSpec((1, rows_d, D_MODEL), lambda i, j: (i, j, 0)),
        ],
        out_specs=[
            pl.BlockSpec((1, D_MODEL, WPREP_TN), lambda i, j: (i, 0, j)),
            pl.BlockSpec((1, rows_d, D_MODEL), lambda i, j: (i, j, 0)),
        ],
        out_shape=[
            jax.ShapeDtypeStruct((e, D_MODEL, 2 * D_FF), BF16),
            jax.ShapeDtypeStruct((e, D_FF, D_MODEL), BF16),
        ],
        compiler_params=pltpu.CompilerParams(
            dimension_semantics=("parallel", "parallel"), vmem_limit_bytes=VMEM_LIMIT),
        name="wprep",
    )(w_gate_up, w_down)


def _expert_kernel(be_ref, tok_ref, h_hbm, wgu_ref, bgu_ref, wd_ref, bd_ref, y_ref, xbuf0, xbuf1, sem):
    i = pl.program_id(0)
    nblk = pl.num_programs(0)
    bufs = (xbuf0, xbuf1)

    def row_copy(blk, r, par):
        t = tok_ref[blk * MOE_BLOCK + r]
        src = h_hbm.at[pl.ds(pl.multiple_of(t * ROW_TILE, ROW_TILE), ROW_TILE), :]
        dst = bufs[par].at[pl.ds(pl.multiple_of(r * ROW_TILE, ROW_TILE), ROW_TILE), :]
        return pltpu.make_async_copy(src, dst, sem.at[par])

    def wait(blk, par):
        def body(r, carry):
            row_copy(blk, r, par).wait()
            return carry
        lax.fori_loop(0, MOE_BLOCK, body, 0, unroll=8)

    @pl.when(i == 0)
    def _():
        def body(r, carry):
            row_copy(0, r, 0).start()
            return carry
        lax.fori_loop(0, MOE_BLOCK, body, 0, unroll=8)

    def step(par):
        wait(i, par)
        nxt = jnp.minimum(i + 1, nblk - 1)
        for r in range(MOE_BLOCK):
            row_copy(nxt, r, 1 - par).start()

        x = _load_row_tiles(bufs[par], MOE_BLOCK).astype(BF16)
        acts = []
        for k in range(2 * D_FF // GU_PAIR):
            sl = slice(k * GU_PAIR, (k + 1) * GU_PAIR)
            gu = _dot(x, wgu_ref[0, :, sl]) + bgu_ref[0, :, sl]
            g = jnp.minimum(gu[:, :LANES], SWIGLU_LIMIT)
            u = jnp.clip(gu[:, LANES:], -SWIGLU_LIMIT, SWIGLU_LIMIT)
            acts.append(((u + 1.0) * (g * jax.nn.sigmoid(SWIGLU_ALPHA * g))).astype(BF16))
        _store_row_tiles(y_ref, _dot(jnp.concatenate(acts, axis=1), wd_ref[0]) + bd_ref[0])

        @pl.when(i == nblk - 1)
        def _():
            wait(nxt, 1 - par)

    for par in range(2):
        pl.when(lax.rem(i, 2) == par)(functools.partial(step, par))


def _experts(block_expert, buf_tok, h2, wgu, bgu, wd, bd):
    n_rows = buf_tok.shape[0]
    n_blocks = n_rows // MOE_BLOCK
    wmap = lambda i, be, tok: (be[i], 0, 0)
    return pl.pallas_call(
        _expert_kernel,
        grid_spec=pltpu.PrefetchScalarGridSpec(
            num_scalar_prefetch=2,
            grid=(n_blocks,),
            in_specs=[
                pl.BlockSpec(memory_space=pl.ANY),
                pl.BlockSpec((1, D_MODEL, 2 * D_FF), wmap),
                pl.BlockSpec((1, 1, 2 * D_FF), wmap),
                pl.BlockSpec((1, D_FF, D_MODEL), wmap),
                pl.BlockSpec((1, 1, D_MODEL), wmap),
            ],
            out_specs=pl.BlockSpec((MOE_BLOCK * ROW_TILE, LANES), lambda i, be, tok: (i, 0)),
            scratch_shapes=[
                pltpu.VMEM((MOE_BLOCK * ROW_TILE, LANES), F32),
                pltpu.VMEM((MOE_BLOCK * ROW_TILE, LANES), F32),
                pltpu.SemaphoreType.DMA((2,)),
            ],
        ),
        out_shape=jax.ShapeDtypeStruct((n_rows * ROW_TILE, LANES), F32),
        compiler_params=pltpu.CompilerParams(
            dimension_semantics=("arbitrary",), vmem_limit_bytes=VMEM_LIMIT),
        name="experts",
    )(block_expert, buf_tok, h2, wgu, bgu, wd, bd)


COMBINE_TM = 128


def _combine_kernel(dest_ref, x1_ref, tw_ref, y_hbm, o_ref, buf0, buf1, sem):
    i = pl.program_id(0)
    nt = pl.num_programs(0)
    tm = COMBINE_TM
    bufs = (buf0, buf1)

    def row_copy(tile, r, k, par):
        d = dest_ref[(tile * tm + r) * TOP_K + k]
        src = y_hbm.at[pl.ds(pl.multiple_of(d * ROW_TILE, ROW_TILE), ROW_TILE), :]
        dst = bufs[par].at[k, pl.ds(pl.multiple_of(r * ROW_TILE, ROW_TILE), ROW_TILE), :]
        return pltpu.make_async_copy(src, dst, sem.at[par])

    def wait(tile, par):
        def body(r, carry):
            for k in range(TOP_K):
                row_copy(tile, r, k, par).wait()
            return carry
        lax.fori_loop(0, tm, body, 0, unroll=2)

    @pl.when(i == 0)
    def _():
        def body(r, carry):
            for k in range(TOP_K):
                row_copy(0, r, k, 0).start()
            return carry
        lax.fori_loop(0, tm, body, 0, unroll=2)

    def step(par):
        wait(i, par)
        nxt = jnp.minimum(i + 1, nt - 1)
        for r in range(tm):
            for k in range(TOP_K):
                row_copy(nxt, r, k, 1 - par).start()

        tw = tw_ref[...]
        gate = [jnp.broadcast_to(tw[:, k:k + 1], (tm, LANES)) for k in range(TOP_K)]
        x1 = x1_ref[...]
        cols = []
        for c in range(ROW_TILE):
            acc = x1[:, c * LANES:(c + 1) * LANES]
            for k in range(TOP_K):
                acc = acc + bufs[par][k, pl.ds(c, tm, stride=ROW_TILE), :] * gate[k]
            cols.append(acc)
        o_ref[...] = jnp.concatenate(cols, axis=1)

        @pl.when(i == nt - 1)
        def _():
            wait(nxt, 1 - par)

    for par in range(2):
        pl.when(lax.rem(i, 2) == par)(functools.partial(step, par))


def _combine(dest_flat, x1, tw, yb):
    t = x1.shape[0]
    tm = COMBINE_TM
    return pl.pallas_call(
        _combine_kernel,
        grid_spec=pltpu.PrefetchScalarGridSpec(
            num_scalar_prefetch=1,
            grid=(t // tm,),
            in_specs=[
                pl.BlockSpec((tm, D_MODEL), lambda i, d: (i, 0)),
                pl.BlockSpec((tm, LANES), lambda i, d: (i, 0)),
                pl.BlockSpec(memory_space=pl.ANY),
            ],
            out_specs=pl.BlockSpec((tm, D_MODEL), lambda i, d: (i, 0)),
            scratch_shapes=[
                pltpu.VMEM((TOP_K, tm * ROW_TILE, LANES), F32),
                pltpu.VMEM((TOP_K, tm * ROW_TILE, LANES), F32),
                pltpu.SemaphoreType.DMA((2,)),
            ],
        ),
        out_shape=jax.ShapeDtypeStruct((t, D_MODEL), F32),
        compiler_params=pltpu.CompilerParams(
            dimension_semantics=("arbitrary",), vmem_limit_bytes=VMEM_LIMIT),
        name="combine",
    )(dest_flat, x1, tw, yb)


def _routing_tables(top_idx, t):
    multi_hot = jnp.sum(jax.nn.one_hot(top_idx, N_EXPERTS, dtype=jnp.int32), axis=1)
    before = jnp.cumsum(multi_hot, axis=0) - multi_hot
    pos = jnp.take_along_axis(before, top_idx, axis=1)
    counts = jnp.sum(multi_hot, axis=0)
    padded = ((counts + MOE_BLOCK - 1) // MOE_BLOCK) * MOE_BLOCK
    cum_padded = jnp.cumsum(padded)
    pstart = cum_padded - padded
    dest = (pstart[top_idx] + pos).astype(jnp.int32)
    n_rows = t * TOP_K + N_EXPERTS * MOE_BLOCK
    n_blocks = n_rows // MOE_BLOCK
    tok = jnp.repeat(jnp.arange(t, dtype=jnp.int32), TOP_K)
    buf_tok = jnp.zeros((n_rows,), jnp.int32).at[dest.reshape(-1)].set(tok)
    block_expert = jnp.minimum(
        jnp.searchsorted(cum_padded, jnp.arange(n_blocks) * MOE_BLOCK, side='right'),
        N_EXPERTS - 1).astype(jnp.int32)
    return dest, buf_tok, block_expert


def kernel(x, ln1_w, w_in, gate_b, q_norm_w, k_norm_w, conv_w, conv_b, dt_bias, a_log, d_skip, ssd_norm_w,
           w_o_attn, w_o_ssd, w_out, ln2_w, w_router, b_router, w_gate_up, b_gate_up, w_down, b_down):
    b, seq, d = x.shape
    t = b * seq
    assert d == D_MODEL and seq % MOBA_BLOCK == 0 and seq % SSD_CHUNK == 0
    assert ln1_w.shape[0] == 1, "single layer"
    x2 = x.reshape(t, d)

    wi = w_in[0]
    col_dt = 3 * ATTN_WIDTH + D_INNER + D_XBC
    w_r = jnp.concatenate(
        [wi[:, :col_dt], wi[:, col_dt + SSD_HEADS:], wi[:, col_dt:col_dt + SSD_HEADS],
         jnp.zeros((d, LANES - SSD_HEADS), wi.dtype)], axis=1).astype(BF16)
    qw2 = jnp.tile(q_norm_w[0], 2)[None, :]
    kw2 = jnp.tile(k_norm_w[0], 2)[None, :]
    pad_g = lambda v: jnp.pad(v.reshape(N_GROUPS, 1, HEADS_PER_GROUP),
                              ((0, 0), (0, 0), (0, LANES - HEADS_PER_GROUP)))
    dtb_g = pad_g(dt_bias[0])
    alog_g = pad_g(a_log[0])
    dskip_e = jnp.repeat(d_skip[0], SSD_HEAD_DIM).reshape(N_GROUPS, 1, GROUP_WIDTH)
    nw_g = ssd_norm_w[0].reshape(N_GROUPS, 1, GROUP_WIDTH)
    wr_hi = w_router[0].astype(BF16)
    wr_lo = (w_router[0] - wr_hi.astype(F32)).astype(BF16)
    zr = jnp.zeros((d, N_EXPERTS), BF16)
    wr_p = jnp.concatenate([jnp.concatenate([wr_hi, wr_lo, zr, zr], axis=1),
                            jnp.concatenate([wr_hi, zr, zr, zr], axis=1)], axis=0)
    br_p = jnp.pad(b_router[0], (0, LANES - N_EXPERTS))[None, :]
    wgu, wd = _wprep(w_gate_up[0], w_down[0])
    bgu = b_gate_up[0].reshape(N_EXPERTS, 2 * D_FF // GU_PAIR, LANES, 2).transpose(0, 1, 3, 2)
    bgu = bgu.reshape(N_EXPERTS, 1, 2 * D_FF)
    bd = b_down[0][:, None, :]

    proj = _inproj(x2, ln1_w, w_r)
    proj3 = proj.reshape(b, seq, NP)
    attn = _moba(proj3, qw2, kw2)
    ssd = _ssd(proj3, conv_w[0], conv_b, dtb_g, alog_g, dskip_e, nw_g)
    x1, h2, ti, tw = _merge(x2, attn.reshape(t, ATTN_WIDTH), ssd.reshape(t, D_INNER), proj, gate_b[0],
                            w_o_attn[0].astype(BF16), w_o_ssd[0].astype(BF16), w_out[0].astype(BF16),
                            ln2_w, wr_p, br_p)

    dest, buf_tok, block_expert = _routing_tables(ti[:, :TOP_K], t)
    yb = _experts(block_expert, buf_tok, h2, wgu, bgu, wd, bd)
    out = _combine(dest.reshape(-1), x1, tw, yb)
    return out.reshape(b, seq, d)
```

```python
import functools

import jax
import jax.numpy as jnp
from jax import lax
from jax.experimental import pallas as pl
from jax.experimental.pallas import tpu as pltpu

F32 = jnp.float32
BF16 = jnp.bfloat16

D_MODEL = 1024
N_HEADS = 16
HEAD_DIM = 64
ATTN_WIDTH = N_HEADS * HEAD_DIM
MOBA_BLOCK = 256
MOBA_TOPK = 3
D_INNER = 2048
SSD_HEAD_DIM = 64
SSD_HEADS = D_INNER // SSD_HEAD_DIM
N_GROUPS = 4
HEADS_PER_GROUP = SSD_HEADS // N_GROUPS
GROUP_WIDTH = HEADS_PER_GROUP * SSD_HEAD_DIM
D_STATE = 128
CONV_K = 4
SSD_CHUNK = 128
D_XBC = D_INNER + 2 * N_GROUPS * D_STATE
N_EXPERTS = 32
TOP_K = 4
D_FF = D_MODEL
SWIGLU_LIMIT = 7.0
SWIGLU_ALPHA = 1.702
MOE_BLOCK = 256
NORM_EPS = 1e-6
NEG_INF = -1e30

LANES = 128
SUBLANES = 8

P_Q = 0
P_Z = 3 * ATTN_WIDTH
P_XBC = P_Z + D_INNER
P_GATE = P_XBC + D_XBC
P_DT = P_GATE + 2 * D_MODEL
NP = P_DT + LANES
PROJ_TN = 1152
VMEM_LIMIT = 56 * 1024 * 1024


def _split3(v):
    hi = v.astype(BF16)
    r1 = v - hi.astype(F32)
    mid = r1.astype(BF16)
    lo = (r1 - mid.astype(F32)).astype(BF16)
    return hi, mid, lo


def _dot(a, b):
    return jnp.dot(a, b, preferred_element_type=F32)


def _dot_nt(a, b):
    return lax.dot_general(a, b, (((1,), (1,)), ((), ())), preferred_element_type=F32)


ROW_TILE = D_MODEL // LANES


def _store_row_tiles(ref, val):
    rows = val.shape[0]
    for c in range(ROW_TILE):
        ref[pl.ds(c, rows, stride=ROW_TILE), :] = val[:, c * LANES:(c + 1) * LANES]


def _load_row_tiles(ref, rows):
    return jnp.concatenate([ref[pl.ds(c, rows, stride=ROW_TILE), :] for c in range(ROW_TILE)], axis=1)


def _inproj_kernel(x_ref, lnw_ref, w_ref, o_ref, h_ref):
    @pl.when(pl.program_id(1) == 0)
    def _():
        x = x_ref[...]
        ms = jnp.mean(x * x, axis=-1, keepdims=True)
        h_ref[...] = (x * lax.rsqrt(ms + NORM_EPS) * lnw_ref[...]).astype(BF16)

    o_ref[...] = _dot(h_ref[...], w_ref[...]).astype(o_ref.dtype)


def _inproj(x2, lnw, w_r):
    t = x2.shape[0]
    tm = min(1024, t)
    return pl.pallas_call(
        _inproj_kernel,
        grid=(t // tm, NP // PROJ_TN),
        in_specs=[
            pl.BlockSpec((tm, D_MODEL), lambda i, j: (i, 0)),
            pl.BlockSpec((1, D_MODEL), lambda i, j: (0, 0)),
            pl.BlockSpec((D_MODEL, PROJ_TN), lambda i, j: (0, j)),
        ],
        out_specs=pl.BlockSpec((tm, PROJ_TN), lambda i, j: (i, j)),
        out_shape=jax.ShapeDtypeStruct((t, NP), BF16),
        scratch_shapes=[pltpu.VMEM((tm, D_MODEL), BF16)],
        compiler_params=pltpu.CompilerParams(
            dimension_semantics=("parallel", "arbitrary"), vmem_limit_bytes=VMEM_LIMIT),
        name="inproj",
    )(x2, lnw, w_r)


def _moba_kernel(q_ref, k_ref, v_ref, qw_ref, kw_ref, o_ref, *, seq, nb):
    lane = lax.broadcasted_iota(jnp.int32, (1, LANES), 1)
    head0 = lane < HEAD_DIM
    r = lax.broadcasted_iota(jnp.int32, (2 * LANES, LANES), 0) % LANES // HEAD_DIM
    c = lax.broadcasted_iota(jnp.int32, (2 * LANES, LANES), 1) // HEAD_DIM
    avg2 = jnp.where(r == c, 1.0 / HEAD_DIM, 0.0).astype(BF16)

    def qk_norm(t_ref, w_ref):
        t = t_ref[0].astype(F32)
        sq = t * t
        hi = sq.astype(BF16)
        lo = (sq - hi.astype(F32)).astype(BF16)
        ms = _dot(jnp.concatenate([hi, lo], axis=1), avg2)
        return t * lax.rsqrt(ms + NORM_EPS) * w_ref[...]

    qn = qk_norm(q_ref, qw_ref)
    kn = qk_norm(k_ref, kw_ref)
    qs = qn * (HEAD_DIM ** -0.5)
    qs_b = qs.astype(BF16)

    kmean = jnp.mean(kn.reshape(nb, MOBA_BLOCK, LANES), axis=1)
    kmx = jnp.concatenate([jnp.where(head0, kmean, 0.0), jnp.where(head0, 0.0, kmean)], axis=0)
    kmx_hi = kmx.astype(BF16)
    kmx_lo = (kmx - kmx_hi.astype(F32)).astype(BF16)
    st = _dot_nt(jnp.concatenate([kmx_hi, kmx_lo], axis=0), qs_b)
    st = st[:2 * nb] + st[2 * nb:]

    qblk = lax.broadcasted_iota(jnp.int32, (nb, seq), 1) // MOBA_BLOCK
    jrow = lax.broadcasted_iota(jnp.int32, (nb, seq), 0)
    past = jrow < qblk
    bias = []
    for a in range(2):
        sm = jnp.where(past, st[a * nb:(a + 1) * nb], NEG_INF)
        rank = jnp.zeros((nb, seq), jnp.int32)
        for jp in range(nb):
            other = sm[jp:jp + 1, :]
            ahead = (other > sm) | ((other == sm) & (jp < jrow))
            rank = rank + ahead.astype(jnp.int32)
        bias.append(jnp.where(past & (rank >= MOBA_TOPK), NEG_INF, 0.0).astype(F32))
    zpad = jnp.zeros((HEAD_DIM - nb, seq), F32)
    bias_t = jnp.concatenate([bias[1], zpad, bias[0], zpad], axis=0).T

    rblk = lax.broadcasted_iota(jnp.int32, (seq, LANES), 0) // MOBA_BLOCK
    l64 = lax.broadcasted_iota(jnp.int32, (seq, LANES), 1) % HEAD_DIM
    ind = (l64 == rblk).astype(F32)

    q_aug = (jnp.where(head0, qs, bias_t).astype(BF16), jnp.where(head0, bias_t, qs).astype(BF16))
    k_aug = (jnp.where(head0, kn, ind).astype(BF16), jnp.where(head0, ind, kn).astype(BF16))
    v = v_ref[0]
    one = jnp.ones((), BF16)
    v_aug = (jnp.where(head0, v, one), jnp.where(head0, one, v))

    tri = (lax.broadcasted_iota(jnp.int32, (MOBA_BLOCK, MOBA_BLOCK), 0)
           >= lax.broadcasted_iota(jnp.int32, (MOBA_BLOCK, MOBA_BLOCK), 1))
    for i in range(nb):
        lo, hi = i * MOBA_BLOCK, (i + 1) * MOBA_BLOCK
        outs = []
        for a in range(2):
            s = _dot_nt(q_aug[a][lo:hi], k_aug[a][:hi])
            own = jnp.where(tri, s[:, lo:], NEG_INF)
            s = own if i == 0 else jnp.concatenate([s[:, :lo], own], axis=1)
            m = jnp.max(s, axis=1, keepdims=True)
            p = jnp.exp(s - m).astype(BF16)
            o = _dot(p, v_aug[a][:hi])
            outs.append(o / pltpu.roll(o, HEAD_DIM, axis=1))
        o_ref[0, lo:hi, :] = jnp.where(head0, outs[0], outs[1]).astype(o_ref.dtype)


def _moba(proj3, qw2, kw2):
    b, seq, _ = proj3.shape
    nb = seq // MOBA_BLOCK
    npair = ATTN_WIDTH // LANES
    return pl.pallas_call(
        functools.partial(_moba_kernel, seq=seq, nb=nb),
        grid=(b, npair),
        in_specs=[
            pl.BlockSpec((1, seq, LANES), lambda i, j: (i, 0, j)),
            pl.BlockSpec((1, seq, LANES), lambda i, j: (i, 0, npair + j)),
            pl.BlockSpec((1, seq, LANES), lambda i, j: (i, 0, 2 * npair + j)),
            pl.BlockSpec((1, LANES), lambda i, j: (0, 0)),
            pl.BlockSpec((1, LANES), lambda i, j: (0, 0)),
        ],
        out_specs=pl.BlockSpec((1, seq, LANES), lambda i, j: (i, 0, j)),
        out_shape=jax.ShapeDtypeStruct((b, seq, ATTN_WIDTH), BF16),
        compiler_params=pltpu.CompilerParams(
            dimension_semantics=("parallel", "parallel"), vmem_limit_bytes=VMEM_LIMIT),
        name="moba",
    )(proj3, proj3, proj3, qw2, kw2)


def _ssd_kernel(x_ref, b_ref, c_ref, z_ref, dt_ref, cwx_ref, cwb_ref, cwc_ref, cbx_ref, cbb_ref, cbc_ref,
                dtb_ref, alog_ref, dskip_ref, nw_ref, o_ref, st_ref, *, nc):
    g = pl.program_id(1)
    ch = SSD_CHUNK
    st_ref[...] = jnp.zeros_like(st_ref)

    ri = lax.broadcasted_iota(jnp.int32, (LANES, LANES), 0)
    ci = lax.broadcasted_iota(jnp.int32, (LANES, LANES), 1)
    sel_g = ((ri == g * HEADS_PER_GROUP + ci) & (ci < HEADS_PER_GROUP)).astype(BF16)
    neg_a = jnp.where(ci[0:1, :] < HEADS_PER_GROUP, -jnp.exp(alog_ref[0]), 0.0)
    tril_b = ri >= ci
    tril = tril_b.astype(BF16)
    tril3 = jnp.concatenate([tril, tril, tril], axis=1)
    er = lax.broadcasted_iota(jnp.int32, (LANES, GROUP_WIDTH), 0)
    ec = lax.broadcasted_iota(jnp.int32, (LANES, GROUP_WIDTH), 1) // SSD_HEAD_DIM
    e1 = (er == ec).astype(BF16)
    e3 = jnp.concatenate([e1, e1, e1], axis=0)
    lane_h0 = lax.broadcasted_iota(jnp.int32, (1, LANES), 1) < SSD_HEAD_DIM
    pack = 2 * SUBLANES

    def chunk(c, carry):
        r0 = pl.multiple_of(c * ch, ch)
        p0 = pl.multiple_of(jnp.maximum(r0 - pack, 0), pack)
        has_prev = c > 0

        def conv_silu(ref, w_ref, bias_ref):
            cur = ref[0, pl.ds(r0, ch), :].astype(F32)
            prev = ref[0, pl.ds(p0, pack), :].astype(F32)[SUBLANES:]
            prev = jnp.where(has_prev, prev, 0.0)
            cat = jnp.concatenate([prev, cur], axis=0)
            w = w_ref[...]
            out = cur * w[CONV_K - 1:CONV_K, :]
            for k in range(1, CONV_K):
                out = out + pltpu.roll(cat, k, axis=0)[SUBLANES:] * w[CONV_K - 1 - k:CONV_K - k, :]
            out = out + bias_ref[...]
            return out * jax.nn.sigmoid(out)

        xs = conv_silu(x_ref, cwx_ref, cbx_ref)
        bm = conv_silu(b_ref, cwb_ref, cbb_ref)
        cm = conv_silu(c_ref, cwc_ref, cbc_ref)
        _ssd_chunk(xs, bm, cm, dt_ref[0, pl.ds(r0, ch), :], z_ref[0, pl.ds(r0, ch), :], sel_g, neg_a, tril_b,
                   tril3, e3, lane_h0, dtb_ref, dskip_ref, nw_ref, o_ref.at[0, pl.ds(r0, ch), :], st_ref)
        return carry

    lax.fori_loop(0, nc, chunk, 0, unroll=2)


def _ssd_chunk(xs, bm, cm, dt_raw, z_raw, sel_g, neg_a, tril_b, tril3, e3, lane_h0, dtb_ref, dskip_ref, nw_ref,
               o_ref, st_ref):
    ch = SSD_CHUNK
    dtr = _dot(dt_raw, sel_g) + dtb_ref[0]
    dtf = jnp.maximum(dtr, 0.0) + jnp.log1p(jnp.exp(-jnp.abs(dtr)))
    a = dtf * neg_a

    a_cum = _dot(tril3, jnp.concatenate(_split3(a), axis=0))
    a_tot = a_cum[ch - 1:ch, :]

    vals = jnp.concatenate([dtf, jnp.exp(a_cum), jnp.exp(a_tot - a_cum)], axis=0)
    ex = _dot(jnp.concatenate(_split3(vals), axis=1), e3)
    dt_e, eac_e, dec_e = ex[:ch], ex[ch:2 * ch], ex[2 * ch:]
    etot_e = eac_e[ch - 1:ch, :]

    xdt = xs * dt_e
    xdt_b = xdt.astype(BF16)
    bm_b = bm.astype(BF16)
    cm_b = cm.astype(BF16)
    cb = _dot_nt(cm_b, bm_b)
    a_cum_t = a_cum.T
    pieces = []
    for hp in range(HEADS_PER_GROUP // 2):
        xpair = xdt_b[:, hp * LANES:(hp + 1) * LANES]
        yd = []
        for hh in range(2):
            h = 2 * hp + hh
            seg = a_cum[:, h:h + 1] - a_cum_t[h:h + 1, :]
            lmat = jnp.exp(jnp.where(tril_b, seg, -jnp.inf))
            yd.append(_dot((cb * lmat).astype(BF16), xpair))
        pieces.append(jnp.where(lane_h0, yd[0], yd[1]))
    y_diag = jnp.concatenate(pieces, axis=1)

    s_prev = st_ref[...]
    y_off = _dot(cm_b, s_prev.astype(BF16)) * eac_e
    st_ref[...] = etot_e * s_prev + _dot(bm.T.astype(BF16), (xdt * dec_e).astype(BF16))

    y = y_diag + y_off + dskip_ref[0] * xs
    z = z_raw.astype(F32)
    gated = y * (z * jax.nn.sigmoid(z))
    ms = jnp.mean(gated * gated, axis=-1, keepdims=True)
    o_ref[...] = (gated * lax.rsqrt(ms + NORM_EPS) * nw_ref[0]).astype(o_ref.dtype)


def _ssd(proj3, conv_w, conv_b2, dtb_g, alog_g, dskip_e, nw_g):
    b, seq, _ = proj3.shape
    nc = seq // SSD_CHUNK
    gw = GROUP_WIDTH
    xo, bo, co, zo = P_XBC // gw, (P_XBC + D_INNER) // LANES, (P_XBC + D_INNER) // LANES + N_GROUPS, P_Z // gw
    return pl.pallas_call(
        functools.partial(_ssd_kernel, nc=nc),
        grid=(b, N_GROUPS),
        in_specs=[
            pl.BlockSpec((1, seq, gw), lambda i, g: (i, 0, xo + g)),
            pl.BlockSpec((1, seq, LANES), lambda i, g: (i, 0, bo + g)),
            pl.BlockSpec((1, seq, LANES), lambda i, g: (i, 0, co + g)),
            pl.BlockSpec((1, seq, gw), lambda i, g: (i, 0, zo + g)),
            pl.BlockSpec((1, seq, LANES), lambda i, g: (i, 0, P_DT // LANES)),
            pl.BlockSpec((CONV_K, gw), lambda i, g: (0, g)),
            pl.BlockSpec((CONV_K, LANES), lambda i, g: (0, D_INNER // LANES + g)),
            pl.BlockSpec((CONV_K, LANES), lambda i, g: (0, D_INNER // LANES + N_GROUPS + g)),
            pl.BlockSpec((1, gw), lambda i, g: (0, g)),
            pl.BlockSpec((1, LANES), lambda i, g: (0, D_INNER // LANES + g)),
            pl.BlockSpec((1, LANES), lambda i, g: (0, D_INNER // LANES + N_GROUPS + g)),
            pl.BlockSpec((1, 1, LANES), lambda i, g: (g, 0, 0)),
            pl.BlockSpec((1, 1, LANES), lambda i, g: (g, 0, 0)),
            pl.BlockSpec((1, 1, gw), lambda i, g: (g, 0, 0)),
            pl.BlockSpec((1, 1, gw), lambda i, g: (g, 0, 0)),
        ],
        out_specs=pl.BlockSpec((1, seq, gw), lambda i, g: (i, 0, g)),
        out_shape=jax.ShapeDtypeStruct((b, seq, D_INNER), BF16),
        scratch_shapes=[pltpu.VMEM((D_STATE, gw), F32)],
        compiler_params=pltpu.CompilerParams(
            dimension_semantics=("parallel", "parallel"), vmem_limit_bytes=VMEM_LIMIT),
        name="ssd",
    )(proj3, proj3, proj3, proj3, proj3, conv_w, conv_w, conv_w, conv_b2, conv_b2, conv_b2,
      dtb_g, alog_g, dskip_e, nw_g)


def _merge_kernel(x_ref, attn_ref, ssd_ref, gl0_ref, gl1_ref, gb_ref, woa_ref, wos_ref, wout_ref,
                  ln2_ref, wr_ref, br_ref, x1_ref, h2_ref, ti_ref, tw_ref, cnt_ref, run_ref):
    ao = _dot(attn_ref[...], woa_ref[...])
    so = _dot(ssd_ref[...], wos_ref[...])
    g0 = jax.nn.sigmoid(gl0_ref[...].astype(F32) + gb_ref[0:1, :])
    g1 = jax.nn.sigmoid(gl1_ref[...].astype(F32) + gb_ref[1:2, :])
    mixed = (g0 * ao + g1 * so).astype(BF16)
    x1 = x_ref[...] + _dot(mixed, wout_ref[...])
    x1_ref[...] = x1
    ms = jnp.mean(x1 * x1, axis=-1, keepdims=True)
    h2 = x1 * lax.rsqrt(ms + NORM_EPS) * ln2_ref[...]
    _store_row_tiles(h2_ref, h2)

    hh = h2.astype(BF16)
    hl = (h2 - hh.astype(F32)).astype(BF16)
    r = _dot(jnp.concatenate([hh, hl], axis=1), wr_ref[...])
    logits = r + pltpu.roll(r, LANES - N_EXPERTS, axis=1) + br_ref[...]
    lane = lax.broadcasted_iota(jnp.int32, logits.shape, 1)
    lane_f = lane.astype(F32)
    cur = jnp.where(lane < N_EXPERTS, logits, -jnp.inf)
    vals, idxs = [], []
    for _ in range(TOP_K):
        m = jnp.max(cur, axis=1, keepdims=True)
        idx = jnp.min(jnp.where(cur == m, lane_f, float(LANES)), axis=1,
                      keepdims=True).astype(jnp.int32)
        vals.append(m)
        idxs.append(idx)
        cur = jnp.where(lane == idx, -jnp.inf, cur)
    es = [jnp.exp(v - vals[0]) for v in vals]
    den = es[0] + es[1] + es[2] + es[3]

    @pl.when(pl.program_id(0) == 0)
    def _():
        run_ref[...] = jnp.zeros_like(run_ref)

    tm = logits.shape[0]
    multi_hot = jnp.zeros(logits.shape, F32)
    for k in range(TOP_K):
        multi_hot = multi_hot + (lane == idxs[k]).astype(F32)
    earlier = (lax.broadcasted_iota(jnp.int32, (tm, tm), 0)
               > lax.broadcasted_iota(jnp.int32, (tm, tm), 1)).astype(BF16)
    before = run_ref[...] + _dot(earlier, multi_hot.astype(BF16))
    run = run_ref[...] + jnp.sum(multi_hot, axis=0, keepdims=True)
    run_ref[...] = run
    cnt_ref[...] = jnp.broadcast_to(run, cnt_ref.shape)

    ti = jnp.zeros(logits.shape, jnp.int32)
    tw = jnp.zeros(logits.shape, F32)
    for k in range(TOP_K):
        pos = jnp.sum(jnp.where(lane == idxs[k], before, 0.0), axis=1, keepdims=True).astype(jnp.int32)
        ti = jnp.where(lane == k, idxs[k], ti)
        ti = jnp.where(lane == TOP_K + k, pos, ti)
        tw = jnp.where(lane == k, es[k] / den, tw)
    ti_ref[...] = ti
    tw_ref[...] = tw


def _merge(x2, attn2, ssd2, proj, gate_b, woa, wos, wout, ln2, wr, br):
    t = x2.shape[0]
    tm = min(512, t)
    const = lambda i: (0, 0)
    gcol = P_GATE // D_MODEL
    return pl.pallas_call(
        _merge_kernel,
        grid=(t // tm,),
        in_specs=[
            pl.BlockSpec((tm, D_MODEL), lambda i: (i, 0)),
            pl.BlockSpec((tm, ATTN_WIDTH), lambda i: (i, 0)),
            pl.BlockSpec((tm, D_INNER), lambda i: (i, 0)),
            pl.BlockSpec((tm, D_MODEL), lambda i: (i, gcol)),
            pl.BlockSpec((tm, D_MODEL), lambda i: (i, gcol + 1)),
            pl.BlockSpec((2, D_MODEL), const),
            pl.BlockSpec((ATTN_WIDTH, D_MODEL), const),
            pl.BlockSpec((D_INNER, D_MODEL), const),
            pl.BlockSpec((D_MODEL, D_MODEL), const),
            pl.BlockSpec((1, D_MODEL), const),
            pl.BlockSpec((2 * D_MODEL, LANES), const),
            pl.BlockSpec((1, LANES), const),
        ],
        out_specs=[
            pl.BlockSpec((tm, D_MODEL), lambda i: (i, 0)),
            pl.BlockSpec((tm * ROW_TILE, LANES), lambda i: (i, 0)),
            pl.BlockSpec((tm, LANES), lambda i: (i, 0)),
            pl.BlockSpec((tm, LANES), lambda i: (i, 0)),
            pl.BlockSpec((SUBLANES, LANES), const),
        ],
        out_shape=[
            jax.ShapeDtypeStruct((t, D_MODEL), F32),
            jax.ShapeDtypeStruct((t * ROW_TILE, LANES), F32),
            jax.ShapeDtypeStruct((t, LANES), jnp.int32),
            jax.ShapeDtypeStruct((t, LANES), F32),
            jax.ShapeDtypeStruct((SUBLANES, LANES), F32),
        ],
        scratch_shapes=[pltpu.VMEM((1, LANES), F32)],
        compiler_params=pltpu.CompilerParams(
            dimension_semantics=("arbitrary",), vmem_limit_bytes=VMEM_LIMIT),
        name="merge",
    )(x2, attn2, ssd2, proj, proj, gate_b, woa, wos, wout, ln2, wr, br)


GU_PAIR = 2 * LANES
WPREP_TN = 512


def _wprep_kernel(wgu_ref, wd_ref, ogu_ref, od_ref):
    r = lax.broadcasted_iota(jnp.int32, (GU_PAIR, GU_PAIR), 0)
    c = lax.broadcasted_iota(jnp.int32, (GU_PAIR, GU_PAIR), 1)
    src = jnp.where(c < LANES, 2 * c, 2 * (c - LANES) + 1)
    perm = (r == src).astype(BF16)
    w = wgu_ref[0].astype(BF16)
    for k in range(WPREP_TN // GU_PAIR):
        sl = slice(k * GU_PAIR, (k + 1) * GU_PAIR)
        ogu_ref[0, :, sl] = _dot(w[:, sl], perm).astype(BF16)
    od_ref[0] = wd_ref[0].astype(BF16)


def _wprep(w_gate_up, w_down):
    e = w_gate_up.shape[0]
    nj = 2 * D_FF // WPREP_TN
    rows_d = D_FF // nj
    return pl.pallas_call(
        _wprep_kernel,
        grid=(e, nj),
        in_specs=[
            pl.BlockSpec((1, D_MODEL, WPREP_TN), lambda i, j: (i, 0, j)),
            pl.BlockSpec((1, rows_d, D_MODEL), lambda i, j: (i, j, 0)),
        ],
        out_specs=[
            pl.BlockSpec((1, D_MODEL, WPREP_TN), lambda i, j: (i, 0, j)),
            pl.BlockSpec((1, rows_d, D_MODEL), lambda i, j: (i, j, 0)),
        ],
        out_shape=[
            jax.ShapeDtypeStruct((e, D_MODEL, 2 * D_FF), BF16),
            jax.ShapeDtypeStruct((e, D_FF, D_MODEL), BF16),
        ],
        compiler_params=pltpu.CompilerParams(
            dimension_semantics=("parallel", "parallel"), vmem_limit_bytes=VMEM_LIMIT),
        name="wprep",
    )(w_gate_up, w_down)


GATHER_DEPTH = 2
GATHER_BUFS = GATHER_DEPTH + 1


def _expert_kernel(be_ref, tok_ref, h_hbm, wgu_ref, bgu_ref, wd_ref, bd_ref, y_ref, *scratch):
    i = pl.program_id(0)
    nblk = pl.num_programs(0)
    bufs, sem = scratch[:GATHER_BUFS], scratch[GATHER_BUFS]

    def row_copy(blk, r, par):
        t = tok_ref[blk * MOE_BLOCK + r]
        src = h_hbm.at[pl.ds(pl.multiple_of(t * ROW_TILE, ROW_TILE), ROW_TILE), :]
        dst = bufs[par].at[pl.ds(pl.multiple_of(r * ROW_TILE, ROW_TILE), ROW_TILE), :]
        return pltpu.make_async_copy(src, dst, sem.at[par])

    def wait(blk, par):
        def body(r, carry):
            row_copy(blk, r, par).wait()
            return carry
        lax.fori_loop(0, MOE_BLOCK, body, 0, unroll=8)

    @pl.when(i == 0)
    def _():
        for d in range(GATHER_DEPTH):
            def body(r, carry, d=d):
                row_copy(jnp.minimum(d, nblk - 1), r, d).start()
                return carry
            lax.fori_loop(0, MOE_BLOCK, body, 0, unroll=8)

    def step(par):
        wait(i, par)
        nxt = jnp.minimum(i + GATHER_DEPTH, nblk - 1)
        npar = (par + GATHER_DEPTH) % GATHER_BUFS
        for r in range(MOE_BLOCK):
            row_copy(nxt, r, npar).start(priority=r % 2)

        x = _load_row_tiles(bufs[par], MOE_BLOCK).astype(BF16)
        acts = []
        for k in range(2 * D_FF // GU_PAIR):
            sl = slice(k * GU_PAIR, (k + 1) * GU_PAIR)
            gu = _dot(x, wgu_ref[0, :, sl]) + bgu_ref[0, :, sl]
            g = jnp.minimum(gu[:, :LANES], SWIGLU_LIMIT)
            u = jnp.clip(gu[:, LANES:], -SWIGLU_LIMIT, SWIGLU_LIMIT)
            acts.append(((u + 1.0) * (g * jax.nn.sigmoid(SWIGLU_ALPHA * g))).astype(BF16))
        _store_row_tiles(y_ref, _dot(jnp.concatenate(acts, axis=1), wd_ref[0]) + bd_ref[0])

        @pl.when(i == nblk - 1)
        def _():
            for d in range(1, GATHER_BUFS):
                wait(nxt, (par + d) % GATHER_BUFS)

    for par in range(GATHER_BUFS):
        pl.when(lax.rem(i, GATHER_BUFS) == par)(functools.partial(step, par))


def _experts(block_expert, buf_tok, h2, wgu, bgu, wd, bd):
    n_rows = buf_tok.shape[0]
    n_blocks = n_rows // MOE_BLOCK
    wmap = lambda i, be, tok: (be[i], 0, 0)
    return pl.pallas_call(
        _expert_kernel,
        grid_spec=pltpu.PrefetchScalarGridSpec(
            num_scalar_prefetch=2,
            grid=(n_blocks,),
            in_specs=[
                pl.BlockSpec(memory_space=pl.ANY),
                pl.BlockSpec((1, D_MODEL, 2 * D_FF), wmap),
                pl.BlockSpec((1, 1, 2 * D_FF), wmap),
                pl.BlockSpec((1, D_FF, D_MODEL), wmap),
                pl.BlockSpec((1, 1, D_MODEL), wmap),
            ],
            out_specs=pl.BlockSpec((MOE_BLOCK * ROW_TILE, LANES), lambda i, be, tok: (i, 0)),
            scratch_shapes=(
                [pltpu.VMEM((MOE_BLOCK * ROW_TILE, LANES), F32)] * GATHER_BUFS
                + [pltpu.SemaphoreType.DMA((GATHER_BUFS,))]),
        ),
        out_shape=jax.ShapeDtypeStruct((n_rows * ROW_TILE, LANES), F32),
        compiler_params=pltpu.CompilerParams(
            dimension_semantics=("arbitrary",), vmem_limit_bytes=VMEM_LIMIT),
        name="experts",
    )(block_expert, buf_tok, h2, wgu, bgu, wd, bd)


COMBINE_TM = 128


def _combine_kernel(dest_ref, x1_ref, tw_ref, y_hbm, o_ref, *scratch):
    i = pl.program_id(0)
    nt = pl.num_programs(0)
    tm = COMBINE_TM
    bufs, sem = scratch[:GATHER_BUFS], scratch[GATHER_BUFS]

    def row_copy(tile, r, k, par):
        d = dest_ref[(tile * tm + r) * TOP_K + k]
        src = y_hbm.at[pl.ds(pl.multiple_of(d * ROW_TILE, ROW_TILE), ROW_TILE), :]
        dst = bufs[par].at[k, pl.ds(pl.multiple_of(r * ROW_TILE, ROW_TILE), ROW_TILE), :]
        return pltpu.make_async_copy(src, dst, sem.at[par])

    def wait(tile, par):
        def body(r, carry):
            for k in range(TOP_K):
                row_copy(tile, r, k, par).wait()
            return carry
        lax.fori_loop(0, tm, body, 0, unroll=2)

    @pl.when(i == 0)
    def _():
        for d in range(GATHER_DEPTH):
            def body(r, carry, d=d):
                for k in range(TOP_K):
                    row_copy(jnp.minimum(d, nt - 1), r, k, d).start()
                return carry
            lax.fori_loop(0, tm, body, 0, unroll=2)

    def step(par):
        wait(i, par)
        nxt = jnp.minimum(i + GATHER_DEPTH, nt - 1)
        npar = (par + GATHER_DEPTH) % GATHER_BUFS
        for r in range(tm):
            for k in range(TOP_K):
                row_copy(nxt, r, k, npar).start(priority=k % 2)

        tw = tw_ref[...]
        gate = [jnp.broadcast_to(tw[:, k:k + 1], (tm, LANES)) for k in range(TOP_K)]
        x1 = x1_ref[...]
        cols = []
        for c in range(ROW_TILE):
            acc = x1[:, c * LANES:(c + 1) * LANES]
            for k in range(TOP_K):
                acc = acc + bufs[par][k, pl.ds(c, tm, stride=ROW_TILE), :] * gate[k]
            cols.append(acc)
        o_ref[...] = jnp.concatenate(cols, axis=1)

        @pl.when(i == nt - 1)
        def _():
            for d in range(1, GATHER_BUFS):
                wait(nxt, (par + d) % GATHER_BUFS)

    for par in range(GATHER_BUFS):
        pl.when(lax.rem(i, GATHER_BUFS) == par)(functools.partial(step, par))


def _combine(dest_flat, x1, tw, yb):
    t = x1.shape[0]
    tm = COMBINE_TM
    return pl.pallas_call(
        _combine_kernel,
        grid_spec=pltpu.PrefetchScalarGridSpec(
            num_scalar_prefetch=1,
            grid=(t // tm,),
            in_specs=[
                pl.BlockSpec((tm, D_MODEL), lambda i, d: (i, 0)),
                pl.BlockSpec((tm, LANES), lambda i, d: (i, 0)),
                pl.BlockSpec(memory_space=pl.ANY),
            ],
            out_specs=pl.BlockSpec((tm, D_MODEL), lambda i, d: (i, 0)),
            scratch_shapes=(
                [pltpu.VMEM((TOP_K, tm * ROW_TILE, LANES), F32)] * GATHER_BUFS
                + [pltpu.SemaphoreType.DMA((GATHER_BUFS,))]),
        ),
        out_shape=jax.ShapeDtypeStruct((t, D_MODEL), F32),
        compiler_params=pltpu.CompilerParams(
            dimension_semantics=("arbitrary",), vmem_limit_bytes=VMEM_LIMIT),
        name="combine",
    )(dest_flat, x1, tw, yb)


def _routing_tables(top_idx, pos, counts, t):
    padded = ((counts + MOE_BLOCK - 1) // MOE_BLOCK) * MOE_BLOCK
    cum_padded = jnp.cumsum(padded)
    pstart = cum_padded - padded
    dest = (pstart[top_idx] + pos).astype(jnp.int32)
    n_rows = t * TOP_K + N_EXPERTS * MOE_BLOCK
    n_blocks = n_rows // MOE_BLOCK
    tok = jnp.repeat(jnp.arange(t, dtype=jnp.int32), TOP_K)
    buf_tok = jnp.zeros((n_rows,), jnp.int32).at[dest.reshape(-1)].set(tok)
    block_expert = jnp.minimum(
        jnp.searchsorted(cum_padded, jnp.arange(n_blocks) * MOE_BLOCK, side='right'),
        N_EXPERTS - 1).astype(jnp.int32)
    return dest, buf_tok, block_expert


def kernel(x, ln1_w, w_in, gate_b, q_norm_w, k_norm_w, conv_w, conv_b, dt_bias, a_log, d_skip, ssd_norm_w,
           w_o_attn, w_o_ssd, w_out, ln2_w, w_router, b_router, w_gate_up, b_gate_up, w_down, b_down):
    b, seq, d = x.shape
    t = b * seq
    assert d == D_MODEL and seq % MOBA_BLOCK == 0 and seq % SSD_CHUNK == 0
    assert ln1_w.shape[0] == 1, "single layer"
    x2 = x.reshape(t, d)

    wi = w_in[0]
    col_dt = 3 * ATTN_WIDTH + D_INNER + D_XBC
    w_r = jnp.concatenate(
        [wi[:, :col_dt], wi[:, col_dt + SSD_HEADS:], wi[:, col_dt:col_dt + SSD_HEADS],
         jnp.zeros((d, LANES - SSD_HEADS), wi.dtype)], axis=1).astype(BF16)
    qw2 = jnp.tile(q_norm_w[0], 2)[None, :]
    kw2 = jnp.tile(k_norm_w[0], 2)[None, :]
    pad_g = lambda v: jnp.pad(v.reshape(N_GROUPS, 1, HEADS_PER_GROUP),
                              ((0, 0), (0, 0), (0, LANES - HEADS_PER_GROUP)))
    dtb_g = pad_g(dt_bias[0])
    alog_g = pad_g(a_log[0])
    dskip_e = jnp.repeat(d_skip[0], SSD_HEAD_DIM).reshape(N_GROUPS, 1, GROUP_WIDTH)
    nw_g = ssd_norm_w[0].reshape(N_GROUPS, 1, GROUP_WIDTH)
    wr_hi = w_router[0].astype(BF16)
    wr_lo = (w_router[0] - wr_hi.astype(F32)).astype(BF16)
    zr = jnp.zeros((d, N_EXPERTS), BF16)
    wr_p = jnp.concatenate([jnp.concatenate([wr_hi, wr_lo, zr, zr], axis=1),
                            jnp.concatenate([wr_hi, zr, zr, zr], axis=1)], axis=0)
    br_p = jnp.pad(b_router[0], (0, LANES - N_EXPERTS))[None, :]
    wgu, wd = _wprep(w_gate_up[0], w_down[0])
    bgu = b_gate_up[0].reshape(N_EXPERTS, 2 * D_FF // GU_PAIR, LANES, 2).transpose(0, 1, 3, 2)
    bgu = bgu.reshape(N_EXPERTS, 1, 2 * D_FF)
    bd = b_down[0][:, None, :]

    proj = _inproj(x2, ln1_w, w_r)
    proj3 = proj.reshape(b, seq, NP)
    attn = _moba(proj3, qw2, kw2)
    ssd = _ssd(proj3, conv_w[0], conv_b, dtb_g, alog_g, dskip_e, nw_g)
    x1, h2, ti, tw, cnt = _merge(x2, attn.reshape(t, ATTN_WIDTH), ssd.reshape(t, D_INNER), proj, gate_b[0],
                                 w_o_attn[0].astype(BF16), w_o_ssd[0].astype(BF16), w_out[0].astype(BF16),
                                 ln2_w, wr_p, br_p)

    dest, buf_tok, block_expert = _routing_tables(
        ti[:, :TOP_K], ti[:, TOP_K:2 * TOP_K], cnt[0, :N_EXPERTS].astype(jnp.int32), t)
    yb = _experts(block_expert, buf_tok, h2, wgu, bgu, wd, bd)
    out = _combine(dest.reshape(-1), x1, tw, yb)
    return out.reshape(b, seq, d)
```

```python
import functools

import jax
import jax.numpy as jnp
from jax import lax
from jax.experimental import pallas as pl
from jax.experimental.pallas import tpu as pltpu

F32 = jnp.float32
BF16 = jnp.bfloat16

D_MODEL = 1024
N_HEADS = 16
HEAD_DIM = 64
ATTN_WIDTH = N_HEADS * HEAD_DIM
MOBA_BLOCK = 256
MOBA_TOPK = 3
D_INNER = 2048
SSD_HEAD_DIM = 64
SSD_HEADS = D_INNER // SSD_HEAD_DIM
N_GROUPS = 4
HEADS_PER_GROUP = SSD_HEADS // N_GROUPS
GROUP_WIDTH = HEADS_PER_GROUP * SSD_HEAD_DIM
D_STATE = 128
CONV_K = 4
SSD_CHUNK = 128
D_XBC = D_INNER + 2 * N_GROUPS * D_STATE
N_EXPERTS = 32
TOP_K = 4
D_FF = D_MODEL
SWIGLU_LIMIT = 7.0
SWIGLU_ALPHA = 1.702
MOE_BLOCK = 256
NORM_EPS = 1e-6
NEG_INF = -1e30

LANES = 128
SUBLANES = 8

P_Q = 0
P_Z = 3 * ATTN_WIDTH
P_XBC = P_Z + D_INNER
P_GATE = P_XBC + D_XBC
P_DT = P_GATE + 2 * D_MODEL
NP = P_DT + LANES
PROJ_TN = 1152
VMEM_LIMIT = 56 * 1024 * 1024


def _split3(v):
    hi = v.astype(BF16)
    r1 = v - hi.astype(F32)
    mid = r1.astype(BF16)
    lo = (r1 - mid.astype(F32)).astype(BF16)
    return hi, mid, lo


def _dot(a, b):
    return jnp.dot(a, b, preferred_element_type=F32)


def _dot_nt(a, b):
    return lax.dot_general(a, b, (((1,), (1,)), ((), ())), preferred_element_type=F32)


ROW_TILE = D_MODEL // LANES


def _store_row_tiles(ref, val):
    rows = val.shape[0]
    for c in range(ROW_TILE):
        ref[pl.ds(c, rows, stride=ROW_TILE), :] = val[:, c * LANES:(c + 1) * LANES]


def _load_row_tiles(ref, rows):
    return jnp.concatenate([ref[pl.ds(c, rows, stride=ROW_TILE), :] for c in range(ROW_TILE)], axis=1)


def _inproj_kernel(x_ref, lnw_ref, w_ref, o_ref, h_ref):
    @pl.when(pl.program_id(1) == 0)
    def _():
        x = x_ref[...]
        ms = jnp.mean(x * x, axis=-1, keepdims=True)
        h_ref[...] = (x * lax.rsqrt(ms + NORM_EPS) * lnw_ref[...]).astype(BF16)

    o_ref[...] = _dot(h_ref[...], w_ref[...]).astype(o_ref.dtype)


def _inproj(x2, lnw, w_r):
    t = x2.shape[0]
    tm = min(1024, t)
    return pl.pallas_call(
        _inproj_kernel,
        grid=(t // tm, NP // PROJ_TN),
        in_specs=[
            pl.BlockSpec((tm, D_MODEL), lambda i, j: (i, 0)),
            pl.BlockSpec((1, D_MODEL), lambda i, j: (0, 0)),
            pl.BlockSpec((D_MODEL, PROJ_TN), lambda i, j: (0, j)),
        ],
        out_specs=pl.BlockSpec((tm, PROJ_TN), lambda i, j: (i, j)),
        out_shape=jax.ShapeDtypeStruct((t, NP), BF16),
        scratch_shapes=[pltpu.VMEM((tm, D_MODEL), BF16)],
        compiler_params=pltpu.CompilerParams(
            dimension_semantics=("parallel", "arbitrary"), vmem_limit_bytes=VMEM_LIMIT),
        name="inproj",
    )(x2, lnw, w_r)


def _moba_kernel(q_ref, k_ref, v_ref, qw_ref, kw_ref, o_ref, *, seq, nb):
    lane = lax.broadcasted_iota(jnp.int32, (1, LANES), 1)
    head0 = lane < HEAD_DIM
    r = lax.broadcasted_iota(jnp.int32, (2 * LANES, LANES), 0) % LANES // HEAD_DIM
    c = lax.broadcasted_iota(jnp.int32, (2 * LANES, LANES), 1) // HEAD_DIM
    avg2 = jnp.where(r == c, 1.0 / HEAD_DIM, 0.0).astype(BF16)

    def qk_norm(t_ref, w_ref):
        t = t_ref[0].astype(F32)
        sq = t * t
        hi = sq.astype(BF16)
        lo = (sq - hi.astype(F32)).astype(BF16)
        ms = _dot(jnp.concatenate([hi, lo], axis=1), avg2)
        return t * lax.rsqrt(ms + NORM_EPS) * w_ref[...]

    qn = qk_norm(q_ref, qw_ref)
    kn = qk_norm(k_ref, kw_ref)
    qs = qn * (HEAD_DIM ** -0.5)
    qs_b = qs.astype(BF16)

    kmean = jnp.mean(kn.reshape(nb, MOBA_BLOCK, LANES), axis=1)
    kmx = jnp.concatenate([jnp.where(head0, kmean, 0.0), jnp.where(head0, 0.0, kmean)], axis=0)
    kmx_hi = kmx.astype(BF16)
    kmx_lo = (kmx - kmx_hi.astype(F32)).astype(BF16)
    st = _dot_nt(jnp.concatenate([kmx_hi, kmx_lo], axis=0), qs_b)
    st = st[:2 * nb] + st[2 * nb:]

    qblk = lax.broadcasted_iota(jnp.int32, (nb, seq), 1) // MOBA_BLOCK
    jrow = lax.broadcasted_iota(jnp.int32, (nb, seq), 0)
    past = jrow < qblk
    bias = []
    for a in range(2):
        sm = jnp.where(past, st[a * nb:(a + 1) * nb], NEG_INF)
        rank = jnp.zeros((nb, seq), jnp.int32)
        for jp in range(nb):
            other = sm[jp:jp + 1, :]
            ahead = (other > sm) | ((other == sm) & (jp < jrow))
            rank = rank + ahead.astype(jnp.int32)
        bias.append(jnp.where(past & (rank >= MOBA_TOPK), NEG_INF, 0.0).astype(F32))
    zpad = jnp.zeros((HEAD_DIM - nb, seq), F32)
    bias_t = jnp.concatenate([bias[1], zpad, bias[0], zpad], axis=0).T

    rblk = lax.broadcasted_iota(jnp.int32, (seq, LANES), 0) // MOBA_BLOCK
    l64 = lax.broadcasted_iota(jnp.int32, (seq, LANES), 1) % HEAD_DIM
    ind = (l64 == rblk).astype(F32)

    q_aug = (jnp.where(head0, qs, bias_t).astype(BF16), jnp.where(head0, bias_t, qs).astype(BF16))
    k_aug = (jnp.where(head0, kn, ind).astype(BF16), jnp.where(head0, ind, kn).astype(BF16))
    v = v_ref[0]
    one = jnp.ones((), BF16)
    v_aug = (jnp.where(head0, v, one), jnp.where(head0, one, v))

    tri = (lax.broadcasted_iota(jnp.int32, (MOBA_BLOCK, MOBA_BLOCK), 0)
           >= lax.broadcasted_iota(jnp.int32, (MOBA_BLOCK, MOBA_BLOCK), 1))
    for i in range(nb):
        lo, hi = i * MOBA_BLOCK, (i + 1) * MOBA_BLOCK
        outs = []
        for a in range(2):
            s = _dot_nt(q_aug[a][lo:hi], k_aug[a][:hi])
            own = jnp.where(tri, s[:, lo:], NEG_INF)
            s = own if i == 0 else jnp.concatenate([s[:, :lo], own], axis=1)
            m = jnp.max(s, axis=1, keepdims=True)
            p = jnp.exp(s - m).astype(BF16)
            o = _dot(p, v_aug[a][:hi])
            outs.append(o / pltpu.roll(o, HEAD_DIM, axis=1))
        o_ref[0, lo:hi, :] = jnp.where(head0, outs[0], outs[1]).astype(o_ref.dtype)


def _moba(proj3, qw2, kw2):
    b, seq, _ = proj3.shape
    nb = seq // MOBA_BLOCK
    npair = ATTN_WIDTH // LANES
    return pl.pallas_call(
        functools.partial(_moba_kernel, seq=seq, nb=nb),
        grid=(b, npair),
        in_specs=[
            pl.BlockSpec((1, seq, LANES), lambda i, j: (i, 0, j)),
            pl.BlockSpec((1, seq, LANES), lambda i, j: (i, 0, npair + j)),
            pl.BlockSpec((1, seq, LANES), lambda i, j: (i, 0, 2 * npair + j)),
            pl.BlockSpec((1, LANES), lambda i, j: (0, 0)),
            pl.BlockSpec((1, LANES), lambda i, j: (0, 0)),
        ],
        out_specs=pl.BlockSpec((1, seq, LANES), lambda i, j: (i, 0, j)),
        out_shape=jax.ShapeDtypeStruct((b, seq, ATTN_WIDTH), BF16),
        compiler_params=pltpu.CompilerParams(
            dimension_semantics=("parallel", "parallel"), vmem_limit_bytes=VMEM_LIMIT),
        name="moba",
    )(proj3, proj3, proj3, qw2, kw2)


def _ssd_kernel(x_ref, b_ref, c_ref, z_ref, dt_ref, cwx_ref, cwb_ref, cwc_ref, cbx_ref, cbb_ref, cbc_ref,
                dtb_ref, alog_ref, dskip_ref, nw_ref, o_ref, st_ref, *, nc):
    g = pl.program_id(1)
    ch = SSD_CHUNK
    st_ref[...] = jnp.zeros_like(st_ref)

    ri = lax.broadcasted_iota(jnp.int32, (LANES, LANES), 0)
    ci = lax.broadcasted_iota(jnp.int32, (LANES, LANES), 1)
    sel_g = ((ri == g * HEADS_PER_GROUP + ci) & (ci < HEADS_PER_GROUP)).astype(BF16)
    neg_a = jnp.where(ci[0:1, :] < HEADS_PER_GROUP, -jnp.exp(alog_ref[0]), 0.0)
    tril_b = ri >= ci
    tril = tril_b.astype(BF16)
    tril3 = jnp.concatenate([tril, tril, tril], axis=1)
    er = lax.broadcasted_iota(jnp.int32, (LANES, GROUP_WIDTH), 0)
    ec = lax.broadcasted_iota(jnp.int32, (LANES, GROUP_WIDTH), 1) // SSD_HEAD_DIM
    e1 = (er == ec).astype(BF16)
    e3 = jnp.concatenate([e1, e1, e1], axis=0)
    lane_h0 = lax.broadcasted_iota(jnp.int32, (1, LANES), 1) < SSD_HEAD_DIM
    pack = 2 * SUBLANES

    def chunk(c, carry):
        r0 = pl.multiple_of(c * ch, ch)
        p0 = pl.multiple_of(jnp.maximum(r0 - pack, 0), pack)
        has_prev = c > 0

        def conv_silu(ref, w_ref, bias_ref):
            cur = ref[0, pl.ds(r0, ch), :].astype(F32)
            prev = ref[0, pl.ds(p0, pack), :].astype(F32)[SUBLANES:]
            prev = jnp.where(has_prev, prev, 0.0)
            cat = jnp.concatenate([prev, cur], axis=0)
            w = w_ref[...]
            out = cur * w[CONV_K - 1:CONV_K, :]
            for k in range(1, CONV_K):
                out = out + pltpu.roll(cat, k, axis=0)[SUBLANES:] * w[CONV_K - 1 - k:CONV_K - k, :]
            out = out + bias_ref[...]
            return out * jax.nn.sigmoid(out)

        xs = conv_silu(x_ref, cwx_ref, cbx_ref)
        bm = conv_silu(b_ref, cwb_ref, cbb_ref)
        cm = conv_silu(c_ref, cwc_ref, cbc_ref)
        _ssd_chunk(xs, bm, cm, dt_ref[0, pl.ds(r0, ch), :], z_ref[0, pl.ds(r0, ch), :], sel_g, neg_a, tril_b,
                   tril3, e3, lane_h0, dtb_ref, dskip_ref, nw_ref, o_ref.at[0, pl.ds(r0, ch), :], st_ref)
        return carry

    lax.fori_loop(0, nc, chunk, 0, unroll=2)


def _ssd_chunk(xs, bm, cm, dt_raw, z_raw, sel_g, neg_a, tril_b, tril3, e3, lane_h0, dtb_ref, dskip_ref, nw_ref,
               o_ref, st_ref):
    ch = SSD_CHUNK
    dtr = _dot(dt_raw, sel_g) + dtb_ref[0]
    dtf = jnp.maximum(dtr, 0.0) + jnp.log1p(jnp.exp(-jnp.abs(dtr)))
    a = dtf * neg_a

    a_cum = _dot(tril3, jnp.concatenate(_split3(a), axis=0))
    a_tot = a_cum[ch - 1:ch, :]

    vals = jnp.concatenate([dtf, jnp.exp(a_cum), jnp.exp(a_tot - a_cum)], axis=0)
    ex = _dot(jnp.concatenate(_split3(vals), axis=1), e3)
    dt_e, eac_e, dec_e = ex[:ch], ex[ch:2 * ch], ex[2 * ch:]
    etot_e = eac_e[ch - 1:ch, :]

    xdt = xs * dt_e
    xdt_b = xdt.astype(BF16)
    bm_b = bm.astype(BF16)
    cm_b = cm.astype(BF16)
    cb = _dot_nt(cm_b, bm_b)
    a_cum_t = a_cum.T
    pieces = []
    for hp in range(HEADS_PER_GROUP // 2):
        xpair = xdt_b[:, hp * LANES:(hp + 1) * LANES]
        yd = []
        for hh in range(2):
            h = 2 * hp + hh
            seg = a_cum[:, h:h + 1] - a_cum_t[h:h + 1, :]
            lmat = jnp.exp(jnp.where(tril_b, seg, -jnp.inf))
            yd.append(_dot((cb * lmat).astype(BF16), xpair))
        pieces.append(jnp.where(lane_h0, yd[0], yd[1]))
    y_diag = jnp.concatenate(pieces, axis=1)

    s_prev = st_ref[...]
    y_off = _dot(cm_b, s_prev.astype(BF16)) * eac_e
    st_ref[...] = etot_e * s_prev + _dot(bm.T.astype(BF16), (xdt * dec_e).astype(BF16))

    y = y_diag + y_off + dskip_ref[0] * xs
    z = z_raw.astype(F32)
    gated = y * (z * jax.nn.sigmoid(z))
    ms = jnp.mean(gated * gated, axis=-1, keepdims=True)
    o_ref[...] = (gated * lax.rsqrt(ms + NORM_EPS) * nw_ref[0]).astype(o_ref.dtype)


def _ssd(proj3, conv_w, conv_b2, dtb_g, alog_g, dskip_e, nw_g):
    b, seq, _ = proj3.shape
    nc = seq // SSD_CHUNK
    gw = GROUP_WIDTH
    xo, bo, co, zo = P_XBC // gw, (P_XBC + D_INNER) // LANES, (P_XBC + D_INNER) // LANES + N_GROUPS, P_Z // gw
    return pl.pallas_call(
        functools.partial(_ssd_kernel, nc=nc),
        grid=(b, N_GROUPS),
        in_specs=[
            pl.BlockSpec((1, seq, gw), lambda i, g: (i, 0, xo + g)),
            pl.BlockSpec((1, seq, LANES), lambda i, g: (i, 0, bo + g)),
            pl.BlockSpec((1, seq, LANES), lambda i, g: (i, 0, co + g)),
            pl.BlockSpec((1, seq, gw), lambda i, g: (i, 0, zo + g)),
            pl.BlockSpec((1, seq, LANES), lambda i, g: (i, 0, P_DT // LANES)),
            pl.BlockSpec((CONV_K, gw), lambda i, g: (0, g)),
            pl.BlockSpec((CONV_K, LANES), lambda i, g: (0, D_INNER // LANES + g)),
            pl.BlockSpec((CONV_K, LANES), lambda i, g: (0, D_INNER // LANES + N_GROUPS + g)),
            pl.BlockSpec((1, gw), lambda i, g: (0, g)),
            pl.BlockSpec((1, LANES), lambda i, g: (0, D_INNER // LANES + g)),
            pl.BlockSpec((1, LANES), lambda i, g: (0, D_INNER // LANES + N_GROUPS + g)),
            pl.BlockSpec((1, 1, LANES), lambda i, g: (g, 0, 0)),
            pl.BlockSpec((1, 1, LANES), lambda i, g: (g, 0, 0)),
            pl.BlockSpec((1, 1, gw), lambda i, g: (g, 0, 0)),
            pl.BlockSpec((1, 1, gw), lambda i, g: (g, 0, 0)),
        ],
        out_specs=pl.BlockSpec((1, seq, gw), lambda i, g: (i, 0, g)),
        out_shape=jax.ShapeDtypeStruct((b, seq, D_INNER), BF16),
        scratch_shapes=[pltpu.VMEM((D_STATE, gw), F32)],
        compiler_params=pltpu.CompilerParams(
            dimension_semantics=("parallel", "parallel"), vmem_limit_bytes=VMEM_LIMIT),
        name="ssd",
    )(proj3, proj3, proj3, proj3, proj3, conv_w, conv_w, conv_w, conv_b2, conv_b2, conv_b2,
      dtb_g, alog_g, dskip_e, nw_g)


def _merge_kernel(x_ref, attn_ref, ssd_ref, gl0_ref, gl1_ref, gb_ref, woa_ref, wos_ref, wout_ref,
                  ln2_ref, wr_ref, br_ref, x1_ref, h2_ref, ti_ref, tw_ref, cnt_ref, run_ref):
    ao = _dot(attn_ref[...], woa_ref[...])
    so = _dot(ssd_ref[...], wos_ref[...])
    g0 = jax.nn.sigmoid(gl0_ref[...].astype(F32) + gb_ref[0:1, :])
    g1 = jax.nn.sigmoid(gl1_ref[...].astype(F32) + gb_ref[1:2, :])
    mixed = (g0 * ao + g1 * so).astype(BF16)
    x1 = x_ref[...] + _dot(mixed, wout_ref[...])
    x1_ref[...] = x1
    ms = jnp.mean(x1 * x1, axis=-1, keepdims=True)
    h2 = x1 * lax.rsqrt(ms + NORM_EPS) * ln2_ref[...]
    _store_row_tiles(h2_ref, h2)

    hh = h2.astype(BF16)
    hl = (h2 - hh.astype(F32)).astype(BF16)
    r = _dot(jnp.concatenate([hh, hl], axis=1), wr_ref[...])
    logits = r + pltpu.roll(r, LANES - N_EXPERTS, axis=1) + br_ref[...]
    lane = lax.broadcasted_iota(jnp.int32, logits.shape, 1)
    lane_f = lane.astype(F32)
    cur = jnp.where(lane < N_EXPERTS, logits, -jnp.inf)
    vals, idxs = [], []
    for _ in range(TOP_K):
        m = jnp.max(cur, axis=1, keepdims=True)
        idx = jnp.min(jnp.where(cur == m, lane_f, float(LANES)), axis=1,
                      keepdims=True).astype(jnp.int32)
        vals.append(m)
        idxs.append(idx)
        cur = jnp.where(lane == idx, -jnp.inf, cur)
    es = [jnp.exp(v - vals[0]) for v in vals]
    den = es[0] + es[1] + es[2] + es[3]

    @pl.when(pl.program_id(0) == 0)
    def _():
        run_ref[...] = jnp.zeros_like(run_ref)

    tm = logits.shape[0]
    multi_hot = jnp.zeros(logits.shape, F32)
    for k in range(TOP_K):
        multi_hot = multi_hot + (lane == idxs[k]).astype(F32)
    earlier = (lax.broadcasted_iota(jnp.int32, (tm, tm), 0)
               > lax.broadcasted_iota(jnp.int32, (tm, tm), 1)).astype(BF16)
    before = run_ref[...] + _dot(earlier, multi_hot.astype(BF16))
    run = run_ref[...] + jnp.sum(multi_hot, axis=0, keepdims=True)
    run_ref[...] = run
    cnt_ref[...] = jnp.broadcast_to(run, cnt_ref.shape)

    ti = jnp.zeros(logits.shape, jnp.int32)
    tw = jnp.zeros(logits.shape, F32)
    for k in range(TOP_K):
        pos = jnp.sum(jnp.where(lane == idxs[k], before, 0.0), axis=1, keepdims=True).astype(jnp.int32)
        ti = jnp.where(lane == k, idxs[k], ti)
        ti = jnp.where(lane == TOP_K + k, pos, ti)
        tw = jnp.where(lane == k, es[k] / den, tw)
    ti_ref[...] = ti
    tw_ref[...] = tw


def _merge(x2, attn2, ssd2, proj, gate_b, woa, wos, wout, ln2, wr, br):
    t = x2.shape[0]
    tm = min(512, t)
    const = lambda i: (0, 0)
    gcol = P_GATE // D_MODEL
    return pl.pallas_call(
        _merge_kernel,
        grid=(t // tm,),
        in_specs=[
            pl.BlockSpec((tm, D_MODEL), lambda i: (i, 0)),
            pl.BlockSpec((tm, ATTN_WIDTH), lambda i: (i, 0)),
            pl.BlockSpec((tm, D_INNER), lambda i: (i, 0)),
            pl.BlockSpec((tm, D_MODEL), lambda i: (i, gcol)),
            pl.BlockSpec((tm, D_MODEL), lambda i: (i, gcol + 1)),
            pl.BlockSpec((2, D_MODEL), const),
            pl.BlockSpec((ATTN_WIDTH, D_MODEL), const),
            pl.BlockSpec((D_INNER, D_MODEL), const),
            pl.BlockSpec((D_MODEL, D_MODEL), const),
            pl.BlockSpec((1, D_MODEL), const),
            pl.BlockSpec((2 * D_MODEL, LANES), const),
            pl.BlockSpec((1, LANES), const),
        ],
        out_specs=[
            pl.BlockSpec((tm, D_MODEL), lambda i: (i, 0)),
            pl.BlockSpec((tm * ROW_TILE, LANES), lambda i: (i, 0)),
            pl.BlockSpec((tm, LANES), lambda i: (i, 0)),
            pl.BlockSpec((tm, LANES), lambda i: (i, 0)),
            pl.BlockSpec((SUBLANES, LANES), const),
        ],
        out_shape=[
            jax.ShapeDtypeStruct((t, D_MODEL), F32),
            jax.ShapeDtypeStruct((t * ROW_TILE, LANES), F32),
            jax.ShapeDtypeStruct((t, LANES), jnp.int32),
            jax.ShapeDtypeStruct((t, LANES), F32),
            jax.ShapeDtypeStruct((SUBLANES, LANES), F32),
        ],
        scratch_shapes=[pltpu.VMEM((1, LANES), F32)],
        compiler_params=pltpu.CompilerParams(
            dimension_semantics=("arbitrary",), vmem_limit_bytes=VMEM_LIMIT),
        name="merge",
    )(x2, attn2, ssd2, proj, proj, gate_b, woa, wos, wout, ln2, wr, br)


GU_PAIR = 2 * LANES
N_GU = 2 * D_FF // GU_PAIR
WPREP_TN = 512


def _wprep_kernel(wgu_ref, wd_ref, ogu_ref, od_ref):
    r = lax.broadcasted_iota(jnp.int32, (GU_PAIR, GU_PAIR), 0)
    c = lax.broadcasted_iota(jnp.int32, (GU_PAIR, GU_PAIR), 1)
    src = jnp.where(c < LANES, 2 * c, 2 * (c - LANES) + 1)
    perm = (r == src).astype(BF16)
    w = wgu_ref[0].astype(BF16)
    for k in range(WPREP_TN // GU_PAIR):
        sl = slice(k * GU_PAIR, (k + 1) * GU_PAIR)
        ogu_ref[0, k] = _dot(w[:, sl], perm).astype(BF16)
    od_ref[0] = wd_ref[0].astype(BF16)


def _wprep(w_gate_up, w_down):
    e = w_gate_up.shape[0]
    nj = 2 * D_FF // WPREP_TN
    rows_d = D_FF // nj
    return pl.pallas_call(
        _wprep_kernel,
        grid=(e, nj),
        in_specs=[
            pl.BlockSpec((1, D_MODEL, WPREP_TN), lambda i, j: (i, 0, j)),
            pl.BlockSpec((1, rows_d, D_MODEL), lambda i, j: (i, j, 0)),
        ],
        out_specs=[
            pl.BlockSpec((1, WPREP_TN // GU_PAIR, D_MODEL, GU_PAIR), lambda i, j: (i, j, 0, 0)),
            pl.BlockSpec((1, rows_d, D_MODEL), lambda i, j: (i, j, 0)),
        ],
        out_shape=[
            jax.ShapeDtypeStruct((e, 2 * D_FF // GU_PAIR, D_MODEL, GU_PAIR), BF16),
            jax.ShapeDtypeStruct((e, D_FF, D_MODEL), BF16),
        ],
        compiler_params=pltpu.CompilerParams(
            dimension_semantics=("parallel", "parallel"), vmem_limit_bytes=VMEM_LIMIT),
        name="wprep",
    )(w_gate_up, w_down)


GATHER_DEPTH = 2
GATHER_BUFS = GATHER_DEPTH + 1


def _expert_kernel(be_ref, tok_ref, h_hbm, wgu_ref, bgu_ref, wd_ref, bd_ref, y_ref, *scratch):
    i = pl.program_id(0)
    nblk = be_ref[pl.num_programs(0)]
    bufs, sem = scratch[:GATHER_BUFS], scratch[GATHER_BUFS]

    def row_copy(blk, r, par):
        t = tok_ref[blk * MOE_BLOCK + r]
        src = h_hbm.at[pl.ds(pl.multiple_of(t * ROW_TILE, ROW_TILE), ROW_TILE), :]
        dst = bufs[par].at[pl.ds(pl.multiple_of(r * ROW_TILE, ROW_TILE), ROW_TILE), :]
        return pltpu.make_async_copy(src, dst, sem.at[par])

    def wait(blk, par):
        def body(r, carry):
            row_copy(blk, r, par).wait()
            return carry
        lax.fori_loop(0, MOE_BLOCK, body, 0, unroll=8)

    @pl.when(i == 0)
    def _():
        for d in range(GATHER_DEPTH):
            def body(r, carry, d=d):
                row_copy(jnp.minimum(d, nblk - 1), r, d).start()
                return carry
            lax.fori_loop(0, MOE_BLOCK, body, 0, unroll=8)

    @pl.when(i >= nblk)
    def _():
        y_ref[...] = jnp.zeros_like(y_ref)

    def step(par):
        wait(i, par)
        nxt = jnp.minimum(i + GATHER_DEPTH, nblk - 1)
        npar = (par + GATHER_DEPTH) % GATHER_BUFS
        for r in range(MOE_BLOCK):
            row_copy(nxt, r, npar).start(priority=r % 2)

        x = _load_row_tiles(bufs[par], MOE_BLOCK).astype(BF16)
        acts = []
        for k in range(N_GU):
            gu = _dot(x, wgu_ref[0, k]) + bgu_ref[0, k]
            g = jnp.minimum(gu[:, :LANES], SWIGLU_LIMIT)
            u = jnp.clip(gu[:, LANES:], -SWIGLU_LIMIT, SWIGLU_LIMIT)
            acts.append(((u + 1.0) * (g * jax.nn.sigmoid(SWIGLU_ALPHA * g))).astype(BF16))
        _store_row_tiles(y_ref, _dot(jnp.concatenate(acts, axis=1), wd_ref[0]) + bd_ref[0])

        @pl.when(i == nblk - 1)
        def _():
            for d in range(1, GATHER_BUFS):
                wait(nxt, (par + d) % GATHER_BUFS)

    for par in range(GATHER_BUFS):
        pl.when((lax.rem(i, GATHER_BUFS) == par) & (i < nblk))(functools.partial(step, par))


def _experts(block_expert, buf_tok, h2, wgu, bgu, wd, bd):
    n_rows = buf_tok.shape[0]
    n_blocks = n_rows // MOE_BLOCK
    wmap = lambda i, be, tok: (be[i], 0, 0)
    return pl.pallas_call(
        _expert_kernel,
        grid_spec=pltpu.PrefetchScalarGridSpec(
            num_scalar_prefetch=2,
            grid=(n_blocks,),
            in_specs=[
                pl.BlockSpec(memory_space=pl.ANY),
                pl.BlockSpec((1, N_GU, D_MODEL, GU_PAIR), lambda i, be, tok: (be[i], 0, 0, 0)),
                pl.BlockSpec((1, N_GU, 1, GU_PAIR), lambda i, be, tok: (be[i], 0, 0, 0)),
                pl.BlockSpec((1, D_FF, D_MODEL), wmap),
                pl.BlockSpec((1, 1, D_MODEL), wmap),
            ],
            out_specs=pl.BlockSpec((MOE_BLOCK * ROW_TILE, LANES), lambda i, be, tok: (i, 0)),
            scratch_shapes=(
                [pltpu.VMEM((MOE_BLOCK * ROW_TILE, LANES), F32)] * GATHER_BUFS
                + [pltpu.SemaphoreType.DMA((GATHER_BUFS,))]),
        ),
        out_shape=jax.ShapeDtypeStruct((n_rows * ROW_TILE, LANES), F32),
        compiler_params=pltpu.CompilerParams(
            dimension_semantics=("arbitrary",), vmem_limit_bytes=VMEM_LIMIT),
        name="experts",
    )(block_expert, buf_tok, h2, wgu, bgu, wd, bd)


COMBINE_TM = 128


def _combine_kernel(dest_ref, x1_ref, tw_ref, y_hbm, o_ref, *scratch):
    i = pl.program_id(0)
    nt = pl.num_programs(0)
    tm = COMBINE_TM
    bufs, sem = scratch[:GATHER_BUFS], scratch[GATHER_BUFS]

    def row_copy(tile, r, k, par):
        d = dest_ref[(tile * tm + r) * TOP_K + k]
        src = y_hbm.at[pl.ds(pl.multiple_of(d * ROW_TILE, ROW_TILE), ROW_TILE), :]
        dst = bufs[par].at[k, pl.ds(pl.multiple_of(r * ROW_TILE, ROW_TILE), ROW_TILE), :]
        return pltpu.make_async_copy(src, dst, sem.at[par])

    def wait(tile, par):
        def body(r, carry):
            for k in range(TOP_K):
                row_copy(tile, r, k, par).wait()
            return carry
        lax.fori_loop(0, tm, body, 0, unroll=2)

    @pl.when(i == 0)
    def _():
        for d in range(GATHER_DEPTH):
            def body(r, carry, d=d):
                for k in range(TOP_K):
                    row_copy(jnp.minimum(d, nt - 1), r, k, d).start()
                return carry
            lax.fori_loop(0, tm, body, 0, unroll=2)

    def step(par):
        wait(i, par)
        nxt = jnp.minimum(i + GATHER_DEPTH, nt - 1)
        npar = (par + GATHER_DEPTH) % GATHER_BUFS
        for r in range(tm):
            for k in range(TOP_K):
                row_copy(nxt, r, k, npar).start(priority=k % 2)

        tw = tw_ref[...]
        gate = [jnp.broadcast_to(tw[:, k:k + 1], (tm, LANES)) for k in range(TOP_K)]
        x1 = x1_ref[...]
        cols = []
        for c in range(ROW_TILE):
            acc = x1[:, c * LANES:(c + 1) * LANES]
            for k in range(TOP_K):
                acc = acc + bufs[par][k, pl.ds(c, tm, stride=ROW_TILE), :] * gate[k]
            cols.append(acc)
        o_ref[...] = jnp.concatenate(cols, axis=1)

        @pl.when(i == nt - 1)
        def _():
            for d in range(1, GATHER_BUFS):
                wait(nxt, (par + d) % GATHER_BUFS)

    for par in range(GATHER_BUFS):
        pl.when(lax.rem(i, GATHER_BUFS) == par)(functools.partial(step, par))


def _combine(dest_flat, x1, tw, yb):
    t = x1.shape[0]
    tm = COMBINE_TM
    return pl.pallas_call(
        _combine_kernel,
        grid_spec=pltpu.PrefetchScalarGridSpec(
            num_scalar_prefetch=1,
            grid=(t // tm,),
            in_specs=[
                pl.BlockSpec((tm, D_MODEL), lambda i, d: (i, 0)),
                pl.BlockSpec((tm, LANES), lambda i, d: (i, 0)),
                pl.BlockSpec(memory_space=pl.ANY),
            ],
            out_specs=pl.BlockSpec((tm, D_MODEL), lambda i, d: (i, 0)),
            scratch_shapes=(
                [pltpu.VMEM((TOP_K, tm * ROW_TILE, LANES), F32)] * GATHER_BUFS
                + [pltpu.SemaphoreType.DMA((GATHER_BUFS,))]),
        ),
        out_shape=jax.ShapeDtypeStruct((t, D_MODEL), F32),
        compiler_params=pltpu.CompilerParams(
            dimension_semantics=("arbitrary",), vmem_limit_bytes=VMEM_LIMIT),
        name="combine",
    )(dest_flat, x1, tw, yb)


ROWTOK_STEPS = 64


def _rowtok_kernel(dest_ref, tok_ref):
    phase, j = pl.program_id(0), pl.program_id(1)
    rows_per = tok_ref.shape[0] // ROWTOK_STEPS
    asg_per = dest_ref.shape[0] // ROWTOK_STEPS

    @pl.when(phase == 0)
    def _():
        def zero(r, carry):
            tok_ref[j * rows_per + r] = 0
            return carry
        lax.fori_loop(0, rows_per, zero, 0, unroll=16)

    @pl.when(phase == 1)
    def _():
        def put(a, carry):
            a = j * asg_per + a
            tok_ref[dest_ref[a]] = lax.shift_right_logical(a, TOP_K.bit_length() - 1)
            return carry
        lax.fori_loop(0, asg_per, put, 0, unroll=16)


def _row_tokens(dest_flat, n_rows):
    assert TOP_K & (TOP_K - 1) == 0
    assert n_rows % ROWTOK_STEPS == 0 and dest_flat.shape[0] % ROWTOK_STEPS == 0
    return pl.pallas_call(
        _rowtok_kernel,
        grid=(2, ROWTOK_STEPS),
        in_specs=[pl.BlockSpec(memory_space=pltpu.SMEM)],
        out_specs=pl.BlockSpec(memory_space=pltpu.SMEM),
        out_shape=jax.ShapeDtypeStruct((n_rows,), jnp.int32),
        compiler_params=pltpu.CompilerParams(dimension_semantics=("arbitrary", "arbitrary")),
        name="rowtok",
    )(dest_flat)


def _routing_tables(top_idx, pos, counts, t):
    padded = ((counts + MOE_BLOCK - 1) // MOE_BLOCK) * MOE_BLOCK
    cum_padded = jnp.cumsum(padded)
    pstart = cum_padded - padded
    dest = (pstart[top_idx] + pos).astype(jnp.int32)
    n_rows = t * TOP_K + N_EXPERTS * MOE_BLOCK
    n_blocks = n_rows // MOE_BLOCK
    buf_tok = _row_tokens(dest.reshape(-1), n_rows)
    block_start = jnp.arange(n_blocks, dtype=jnp.int32) * MOE_BLOCK
    block_expert = jnp.minimum(
        jnp.sum((cum_padded[None, :] <= block_start[:, None]).astype(jnp.int32), axis=1), N_EXPERTS - 1)
    block_expert = jnp.concatenate([block_expert, cum_padded[-1:] // MOE_BLOCK]).astype(jnp.int32)
    return dest, buf_tok, block_expert


def kernel(x, ln1_w, w_in, gate_b, q_norm_w, k_norm_w, conv_w, conv_b, dt_bias, a_log, d_skip, ssd_norm_w,
           w_o_attn, w_o_ssd, w_out, ln2_w, w_router, b_router, w_gate_up, b_gate_up, w_down, b_down):
    b, seq, d = x.shape
    t = b * seq
    assert d == D_MODEL and seq % MOBA_BLOCK == 0 and seq % SSD_CHUNK == 0
    assert ln1_w.shape[0] == 1, "single layer"
    x2 = x.reshape(t, d)

    wi = w_in[0]
    col_dt = 3 * ATTN_WIDTH + D_INNER + D_XBC
    w_r = jnp.concatenate(
        [wi[:, :col_dt], wi[:, col_dt + SSD_HEADS:], wi[:, col_dt:col_dt + SSD_HEADS],
         jnp.zeros((d, LANES - SSD_HEADS), wi.dtype)], axis=1).astype(BF16)
    qw2 = jnp.tile(q_norm_w[0], 2)[None, :]
    kw2 = jnp.tile(k_norm_w[0], 2)[None, :]
    pad_g = lambda v: jnp.pad(v.reshape(N_GROUPS, 1, HEADS_PER_GROUP),
                              ((0, 0), (0, 0), (0, LANES - HEADS_PER_GROUP)))
    dtb_g = pad_g(dt_bias[0])
    alog_g = pad_g(a_log[0])
    dskip_e = jnp.repeat(d_skip[0], SSD_HEAD_DIM).reshape(N_GROUPS, 1, GROUP_WIDTH)
    nw_g = ssd_norm_w[0].reshape(N_GROUPS, 1, GROUP_WIDTH)
    wr_hi = w_router[0].astype(BF16)
    wr_lo = (w_router[0] - wr_hi.astype(F32)).astype(BF16)
    zr = jnp.zeros((d, N_EXPERTS), BF16)
    wr_p = jnp.concatenate([jnp.concatenate([wr_hi, wr_lo, zr, zr], axis=1),
                            jnp.concatenate([wr_hi, zr, zr, zr], axis=1)], axis=0)
    br_p = jnp.pad(b_router[0], (0, LANES - N_EXPERTS))[None, :]
    wgu, wd = _wprep(w_gate_up[0], w_down[0])
    bgu = b_gate_up[0].reshape(N_EXPERTS, 2 * D_FF // GU_PAIR, LANES, 2).transpose(0, 1, 3, 2)
    bgu = bgu.reshape(N_EXPERTS, N_GU, 1, GU_PAIR)
    bd = b_down[0][:, None, :]

    proj = _inproj(x2, ln1_w, w_r)
    proj3 = proj.reshape(b, seq, NP)
    attn = _moba(proj3, qw2, kw2)
    ssd = _ssd(proj3, conv_w[0], conv_b, dtb_g, alog_g, dskip_e, nw_g)
    x1, h2, ti, tw, cnt = _merge(x2, attn.reshape(t, ATTN_WIDTH), ssd.reshape(t, D_INNER), proj, gate_b[0],
                                 w_o_attn[0].astype(BF16), w_o_ssd[0].astype(BF16), w_out[0].astype(BF16),
                                 ln2_w, wr_p, br_p)

    dest, buf_tok, block_expert = _routing_tables(
        ti[:, :TOP_K], ti[:, TOP_K:2 * TOP_K], cnt[0, :N_EXPERTS].astype(jnp.int32), t)
    yb = _experts(block_expert, buf_tok, h2, wgu, bgu, wd, bd)
    out = _combine(dest.reshape(-1), x1, tw, yb)
    return out.reshape(b, seq, d)
```

```python
import functools

import jax
import jax.numpy as jnp
from jax import lax
from jax.experimental import pallas as pl
from jax.experimental.pallas import tpu as pltpu

F32 = jnp.float32
BF16 = jnp.bfloat16

D_MODEL = 1024
N_HEADS = 16
HEAD_DIM = 64
ATTN_WIDTH = N_HEADS * HEAD_DIM
MOBA_BLOCK = 256
MOBA_TOPK = 3
D_INNER = 2048
SSD_HEAD_DIM = 64
SSD_HEADS = D_INNER // SSD_HEAD_DIM
N_GROUPS = 4
HEADS_PER_GROUP = SSD_HEADS // N_GROUPS
GROUP_WIDTH = HEADS_PER_GROUP * SSD_HEAD_DIM
D_STATE = 128
CONV_K = 4
SSD_CHUNK = 128
D_XBC = D_INNER + 2 * N_GROUPS * D_STATE
N_EXPERTS = 32
TOP_K = 4
D_FF = D_MODEL
SWIGLU_LIMIT = 7.0
SWIGLU_ALPHA = 1.702
MOE_BLOCK = 256
NORM_EPS = 1e-6
NEG_INF = -1e30

LANES = 128
SUBLANES = 8

P_Q = 0
P_Z = 3 * ATTN_WIDTH
P_XBC = P_Z + D_INNER
P_GATE = P_XBC + D_XBC
P_DT = P_GATE + 2 * D_MODEL
NP = P_DT + LANES
PROJ_TN = 1152
VMEM_LIMIT = 56 * 1024 * 1024


def _split3(v):
    hi = v.astype(BF16)
    r1 = v - hi.astype(F32)
    mid = r1.astype(BF16)
    lo = (r1 - mid.astype(F32)).astype(BF16)
    return hi, mid, lo


def _dot(a, b):
    return jnp.dot(a, b, preferred_element_type=F32)


def _dot_nt(a, b):
    return lax.dot_general(a, b, (((1,), (1,)), ((), ())), preferred_element_type=F32)


ROW_TILE = D_MODEL // LANES


def _store_row_tiles(ref, val):
    rows = val.shape[0]
    for c in range(ROW_TILE):
        ref[pl.ds(c, rows, stride=ROW_TILE), :] = val[:, c * LANES:(c + 1) * LANES]


def _load_row_tiles(ref, rows):
    return jnp.concatenate([ref[pl.ds(c, rows, stride=ROW_TILE), :] for c in range(ROW_TILE)], axis=1)


def _inproj_kernel(x_ref, lnw_ref, w_ref, o_ref, h_ref):
    @pl.when(pl.program_id(1) == 0)
    def _():
        x = x_ref[...]
        ms = jnp.mean(x * x, axis=-1, keepdims=True)
        h_ref[...] = (x * lax.rsqrt(ms + NORM_EPS) * lnw_ref[...]).astype(BF16)

    o_ref[...] = _dot(h_ref[...], w_ref[...]).astype(o_ref.dtype)


def _inproj(x2, lnw, w_r):
    t = x2.shape[0]
    tm = min(1024, t)
    return pl.pallas_call(
        _inproj_kernel,
        grid=(t // tm, NP // PROJ_TN),
        in_specs=[
            pl.BlockSpec((tm, D_MODEL), lambda i, j: (i, 0)),
            pl.BlockSpec((1, D_MODEL), lambda i, j: (0, 0)),
            pl.BlockSpec((D_MODEL, PROJ_TN), lambda i, j: (0, j)),
        ],
        out_specs=pl.BlockSpec((tm, PROJ_TN), lambda i, j: (i, j)),
        out_shape=jax.ShapeDtypeStruct((t, NP), BF16),
        scratch_shapes=[pltpu.VMEM((tm, D_MODEL), BF16)],
        compiler_params=pltpu.CompilerParams(
            dimension_semantics=("parallel", "arbitrary"), vmem_limit_bytes=VMEM_LIMIT),
        name="inproj",
    )(x2, lnw, w_r)


def _moba_kernel(q_ref, k_ref, v_ref, qw_ref, kw_ref, o_ref, *, seq, nb):
    lane = lax.broadcasted_iota(jnp.int32, (1, LANES), 1)
    head0 = lane < HEAD_DIM
    r = lax.broadcasted_iota(jnp.int32, (2 * LANES, LANES), 0) % LANES // HEAD_DIM
    c = lax.broadcasted_iota(jnp.int32, (2 * LANES, LANES), 1) // HEAD_DIM
    avg2 = jnp.where(r == c, 1.0 / HEAD_DIM, 0.0).astype(BF16)

    def qk_norm(t_ref, w_ref):
        t = t_ref[0].astype(F32)
        sq = t * t
        hi = sq.astype(BF16)
        lo = (sq - hi.astype(F32)).astype(BF16)
        ms = _dot(jnp.concatenate([hi, lo], axis=1), avg2)
        return t * lax.rsqrt(ms + NORM_EPS) * w_ref[...]

    qn = qk_norm(q_ref, qw_ref)
    kn = qk_norm(k_ref, kw_ref)
    qs = qn * (HEAD_DIM ** -0.5)
    qs_b = qs.astype(BF16)

    kmean = jnp.mean(kn.reshape(nb, MOBA_BLOCK, LANES), axis=1)
    kmx = jnp.concatenate([jnp.where(head0, kmean, 0.0), jnp.where(head0, 0.0, kmean)], axis=0)
    kmx_hi = kmx.astype(BF16)
    kmx_lo = (kmx - kmx_hi.astype(F32)).astype(BF16)
    st = _dot_nt(jnp.concatenate([kmx_hi, kmx_lo], axis=0), qs_b)
    st = st[:2 * nb] + st[2 * nb:]

    qblk = lax.broadcasted_iota(jnp.int32, (nb, seq), 1) // MOBA_BLOCK
    jrow = lax.broadcasted_iota(jnp.int32, (nb, seq), 0)
    past = jrow < qblk
    bias = []
    for a in range(2):
        sm = jnp.where(past, st[a * nb:(a + 1) * nb], NEG_INF)
        rank = jnp.zeros((nb, seq), jnp.int32)
        for jp in range(nb):
            other = sm[jp:jp + 1, :]
            ahead = (other > sm) | ((other == sm) & (jp < jrow))
            rank = rank + ahead.astype(jnp.int32)
        bias.append(jnp.where(past & (rank >= MOBA_TOPK), NEG_INF, 0.0).astype(F32))
    zpad = jnp.zeros((HEAD_DIM - nb, seq), F32)
    bias_t = jnp.concatenate([bias[1], zpad, bias[0], zpad], axis=0).T

    rblk = lax.broadcasted_iota(jnp.int32, (seq, LANES), 0) // MOBA_BLOCK
    l64 = lax.broadcasted_iota(jnp.int32, (seq, LANES), 1) % HEAD_DIM
    ind = (l64 == rblk).astype(F32)

    q_aug = (jnp.where(head0, qs, bias_t).astype(BF16), jnp.where(head0, bias_t, qs).astype(BF16))
    k_aug = (jnp.where(head0, kn, ind).astype(BF16), jnp.where(head0, ind, kn).astype(BF16))
    v = v_ref[0]
    one = jnp.ones((), BF16)
    v_aug = (jnp.where(head0, v, one), jnp.where(head0, one, v))

    tri = (lax.broadcasted_iota(jnp.int32, (MOBA_BLOCK, MOBA_BLOCK), 0)
           >= lax.broadcasted_iota(jnp.int32, (MOBA_BLOCK, MOBA_BLOCK), 1))
    for i in range(nb):
        lo, hi = i * MOBA_BLOCK, (i + 1) * MOBA_BLOCK
        outs = []
        for a in range(2):
            s = _dot_nt(q_aug[a][lo:hi], k_aug[a][:hi])
            own = jnp.where(tri, s[:, lo:], NEG_INF)
            s = own if i == 0 else jnp.concatenate([s[:, :lo], own], axis=1)
            m = jnp.max(s, axis=1, keepdims=True)
            p = jnp.exp(s - m).astype(BF16)
            o = _dot(p, v_aug[a][:hi])
            outs.append(o / pltpu.roll(o, HEAD_DIM, axis=1))
        o_ref[0, lo:hi, :] = jnp.where(head0, outs[0], outs[1]).astype(o_ref.dtype)


def _moba(proj3, qw2, kw2):
    b, seq, _ = proj3.shape
    nb = seq // MOBA_BLOCK
    npair = ATTN_WIDTH // LANES
    return pl.pallas_call(
        functools.partial(_moba_kernel, seq=seq, nb=nb),
        grid=(b, npair),
        in_specs=[
            pl.BlockSpec((1, seq, LANES), lambda i, j: (i, 0, j)),
            pl.BlockSpec((1, seq, LANES), lambda i, j: (i, 0, npair + j)),
            pl.BlockSpec((1, seq, LANES), lambda i, j: (i, 0, 2 * npair + j)),
            pl.BlockSpec((1, LANES), lambda i, j: (0, 0)),
            pl.BlockSpec((1, LANES), lambda i, j: (0, 0)),
        ],
        out_specs=pl.BlockSpec((1, seq, LANES), lambda i, j: (i, 0, j)),
        out_shape=jax.ShapeDtypeStruct((b, seq, ATTN_WIDTH), BF16),
        compiler_params=pltpu.CompilerParams(
            dimension_semantics=("parallel", "parallel"), vmem_limit_bytes=VMEM_LIMIT),
        name="moba",
    )(proj3, proj3, proj3, qw2, kw2)


def _ssd_kernel(x_ref, b_ref, c_ref, z_ref, dt_ref, cwx_ref, cwb_ref, cwc_ref, cbx_ref, cbb_ref, cbc_ref,
                dtb_ref, alog_ref, dskip_ref, nw_ref, o_ref, st_ref, *, nc):
    g = pl.program_id(1)
    ch = SSD_CHUNK
    st_ref[...] = jnp.zeros_like(st_ref)

    ri = lax.broadcasted_iota(jnp.int32, (LANES, LANES), 0)
    ci = lax.broadcasted_iota(jnp.int32, (LANES, LANES), 1)
    sel_g = ((ri == g * HEADS_PER_GROUP + ci) & (ci < HEADS_PER_GROUP)).astype(BF16)
    neg_a = jnp.where(ci[0:1, :] < HEADS_PER_GROUP, -jnp.exp(alog_ref[0]), 0.0)
    tril_b = ri >= ci
    tril = tril_b.astype(BF16)
    tril3 = jnp.concatenate([tril, tril, tril], axis=1)
    er = lax.broadcasted_iota(jnp.int32, (LANES, GROUP_WIDTH), 0)
    ec = lax.broadcasted_iota(jnp.int32, (LANES, GROUP_WIDTH), 1) // SSD_HEAD_DIM
    e1 = (er == ec).astype(BF16)
    e3 = jnp.concatenate([e1, e1, e1], axis=0)
    lane_h0 = lax.broadcasted_iota(jnp.int32, (1, LANES), 1) < SSD_HEAD_DIM
    pack = 2 * SUBLANES

    def chunk(c, carry):
        r0 = pl.multiple_of(c * ch, ch)
        p0 = pl.multiple_of(jnp.maximum(r0 - pack, 0), pack)
        has_prev = c > 0

        def conv_silu(ref, w_ref, bias_ref):
            cur = ref[0, pl.ds(r0, ch), :].astype(F32)
            prev = ref[0, pl.ds(p0, pack), :].astype(F32)[SUBLANES:]
            prev = jnp.where(has_prev, prev, 0.0)
            cat = jnp.concatenate([prev, cur], axis=0)
            w = w_ref[...]
            out = cur * w[CONV_K - 1:CONV_K, :]
            for k in range(1, CONV_K):
                out = out + pltpu.roll(cat, k, axis=0)[SUBLANES:] * w[CONV_K - 1 - k:CONV_K - k, :]
            out = out + bias_ref[...]
            return out * jax.nn.sigmoid(out)

        xs = conv_silu(x_ref, cwx_ref, cbx_ref)
        bm = conv_silu(b_ref, cwb_ref, cbb_ref)
        cm = conv_silu(c_ref, cwc_ref, cbc_ref)
        _ssd_chunk(xs, bm, cm, dt_ref[0, pl.ds(r0, ch), :], z_ref[0, pl.ds(r0, ch), :], sel_g, neg_a, tril_b,
                   tril3, e3, lane_h0, dtb_ref, dskip_ref, nw_ref, o_ref.at[0, pl.ds(r0, ch), :], st_ref)
        return carry

    lax.fori_loop(0, nc, chunk, 0, unroll=2)


def _ssd_chunk(xs, bm, cm, dt_raw, z_raw, sel_g, neg_a, tril_b, tril3, e3, lane_h0, dtb_ref, dskip_ref, nw_ref,
               o_ref, st_ref):
    ch = SSD_CHUNK
    dtr = _dot(dt_raw, sel_g) + dtb_ref[0]
    dtf = jnp.maximum(dtr, 0.0) + jnp.log1p(jnp.exp(-jnp.abs(dtr)))
    a = dtf * neg_a

    a_cum = _dot(tril3, jnp.concatenate(_split3(a), axis=0))
    a_tot = a_cum[ch - 1:ch, :]

    vals = jnp.concatenate([dtf, jnp.exp(a_cum), jnp.exp(a_tot - a_cum)], axis=0)
    ex = _dot(jnp.concatenate(_split3(vals), axis=1), e3)
    dt_e, eac_e, dec_e = ex[:ch], ex[ch:2 * ch], ex[2 * ch:]
    etot_e = eac_e[ch - 1:ch, :]

    xdt = xs * dt_e
    xdt_b = xdt.astype(BF16)
    bm_b = bm.astype(BF16)
    cm_b = cm.astype(BF16)
    cb = _dot_nt(cm_b, bm_b)
    a_cum_t = a_cum.T
    pieces = []
    for hp in range(HEADS_PER_GROUP // 2):
        xpair = xdt_b[:, hp * LANES:(hp + 1) * LANES]
        yd = []
        for hh in range(2):
            h = 2 * hp + hh
            seg = a_cum[:, h:h + 1] - a_cum_t[h:h + 1, :]
            lmat = jnp.exp(jnp.where(tril_b, seg, -jnp.inf))
            yd.append(_dot((cb * lmat).astype(BF16), xpair))
        pieces.append(jnp.where(lane_h0, yd[0], yd[1]))
    y_diag = jnp.concatenate(pieces, axis=1)

    s_prev = st_ref[...]
    y_off = _dot(cm_b, s_prev.astype(BF16)) * eac_e
    st_ref[...] = etot_e * s_prev + _dot(bm.T.astype(BF16), (xdt * dec_e).astype(BF16))

    y = y_diag + y_off + dskip_ref[0] * xs
    z = z_raw.astype(F32)
    gated = y * (z * jax.nn.sigmoid(z))
    ms = jnp.mean(gated * gated, axis=-1, keepdims=True)
    o_ref[...] = (gated * lax.rsqrt(ms + NORM_EPS) * nw_ref[0]).astype(o_ref.dtype)


def _ssd(proj3, conv_w, conv_b2, dtb_g, alog_g, dskip_e, nw_g):
    b, seq, _ = proj3.shape
    nc = seq // SSD_CHUNK
    gw = GROUP_WIDTH
    xo, bo, co, zo = P_XBC // gw, (P_XBC + D_INNER) // LANES, (P_XBC + D_INNER) // LANES + N_GROUPS, P_Z // gw
    return pl.pallas_call(
        functools.partial(_ssd_kernel, nc=nc),
        grid=(b, N_GROUPS),
        in_specs=[
            pl.BlockSpec((1, seq, gw), lambda i, g: (i, 0, xo + g)),
            pl.BlockSpec((1, seq, LANES), lambda i, g: (i, 0, bo + g)),
            pl.BlockSpec((1, seq, LANES), lambda i, g: (i, 0, co + g)),
            pl.BlockSpec((1, seq, gw), lambda i, g: (i, 0, zo + g)),
            pl.BlockSpec((1, seq, LANES), lambda i, g: (i, 0, P_DT // LANES)),
            pl.BlockSpec((CONV_K, gw), lambda i, g: (0, g)),
            pl.BlockSpec((CONV_K, LANES), lambda i, g: (0, D_INNER // LANES + g)),
            pl.BlockSpec((CONV_K, LANES), lambda i, g: (0, D_INNER // LANES + N_GROUPS + g)),
            pl.BlockSpec((1, gw), lambda i, g: (0, g)),
            pl.BlockSpec((1, LANES), lambda i, g: (0, D_INNER // LANES + g)),
            pl.BlockSpec((1, LANES), lambda i, g: (0, D_INNER // LANES + N_GROUPS + g)),
            pl.BlockSpec((1, 1, LANES), lambda i, g: (g, 0, 0)),
            pl.BlockSpec((1, 1, LANES), lambda i, g: (g, 0, 0)),
            pl.BlockSpec((1, 1, gw), lambda i, g: (g, 0, 0)),
            pl.BlockSpec((1, 1, gw), lambda i, g: (g, 0, 0)),
        ],
        out_specs=pl.BlockSpec((1, seq, gw), lambda i, g: (i, 0, g)),
        out_shape=jax.ShapeDtypeStruct((b, seq, D_INNER), BF16),
        scratch_shapes=[pltpu.VMEM((D_STATE, gw), F32)],
        compiler_params=pltpu.CompilerParams(
            dimension_semantics=("parallel", "parallel"), vmem_limit_bytes=VMEM_LIMIT),
        name="ssd",
    )(proj3, proj3, proj3, proj3, proj3, conv_w, conv_w, conv_w, conv_b2, conv_b2, conv_b2,
      dtb_g, alog_g, dskip_e, nw_g)


def _merge_kernel(x_ref, attn_ref, ssd_ref, gl0_ref, gl1_ref, gb_ref, woa_ref, wos_ref, wout_ref,
                  ln2_ref, wr_ref, br_ref, x1_ref, h2_ref, ti_ref, tw_ref, cnt_ref, run_ref):
    ao = _dot(attn_ref[...], woa_ref[...])
    so = _dot(ssd_ref[...], wos_ref[...])
    g0 = jax.nn.sigmoid(gl0_ref[...].astype(F32) + gb_ref[0:1, :])
    g1 = jax.nn.sigmoid(gl1_ref[...].astype(F32) + gb_ref[1:2, :])
    mixed = (g0 * ao + g1 * so).astype(BF16)
    x1 = x_ref[...] + _dot(mixed, wout_ref[...])
    x1_ref[...] = x1
    ms = jnp.mean(x1 * x1, axis=-1, keepdims=True)
    h2 = x1 * lax.rsqrt(ms + NORM_EPS) * ln2_ref[...]
    _store_row_tiles(h2_ref, h2)

    hh = h2.astype(BF16)
    hl = (h2 - hh.astype(F32)).astype(BF16)
    r = _dot(jnp.concatenate([hh, hl], axis=1), wr_ref[...])
    logits = r + pltpu.roll(r, LANES - N_EXPERTS, axis=1) + br_ref[...]
    lane = lax.broadcasted_iota(jnp.int32, logits.shape, 1)
    lane_f = lane.astype(F32)
    cur = jnp.where(lane < N_EXPERTS, logits, -jnp.inf)
    vals, idxs = [], []
    for _ in range(TOP_K):
        m = jnp.max(cur, axis=1, keepdims=True)
        idx = jnp.min(jnp.where(cur == m, lane_f, float(LANES)), axis=1,
                      keepdims=True).astype(jnp.int32)
        vals.append(m)
        idxs.append(idx)
        cur = jnp.where(lane == idx, -jnp.inf, cur)
    es = [jnp.exp(v - vals[0]) for v in vals]
    den = es[0] + es[1] + es[2] + es[3]

    @pl.when(pl.program_id(0) == 0)
    def _():
        run_ref[...] = jnp.zeros_like(run_ref)

    tm = logits.shape[0]
    multi_hot = jnp.zeros(logits.shape, F32)
    for k in range(TOP_K):
        multi_hot = multi_hot + (lane == idxs[k]).astype(F32)
    earlier = (lax.broadcasted_iota(jnp.int32, (tm, tm), 0)
               > lax.broadcasted_iota(jnp.int32, (tm, tm), 1)).astype(BF16)
    before = run_ref[...] + _dot(earlier, multi_hot.astype(BF16))
    run = run_ref[...] + jnp.sum(multi_hot, axis=0, keepdims=True)
    run_ref[...] = run
    cnt_ref[...] = jnp.broadcast_to(run, cnt_ref.shape)

    ti = jnp.zeros(logits.shape, jnp.int32)
    tw = jnp.zeros(logits.shape, F32)
    for k in range(TOP_K):
        pos = jnp.sum(jnp.where(lane == idxs[k], before, 0.0), axis=1, keepdims=True).astype(jnp.int32)
        ti = jnp.where(lane == k, idxs[k], ti)
        ti = jnp.where(lane == TOP_K + k, pos, ti)
        tw = jnp.where(lane == k, es[k] / den, tw)
    ti_ref[...] = ti
    tw_ref[...] = tw


def _merge(x2, attn2, ssd2, proj, gate_b, woa, wos, wout, ln2, wr, br):
    t = x2.shape[0]
    tm = min(512, t)
    const = lambda i: (0, 0)
    gcol = P_GATE // D_MODEL
    return pl.pallas_call(
        _merge_kernel,
        grid=(t // tm,),
        in_specs=[
            pl.BlockSpec((tm, D_MODEL), lambda i: (i, 0)),
            pl.BlockSpec((tm, ATTN_WIDTH), lambda i: (i, 0)),
            pl.BlockSpec((tm, D_INNER), lambda i: (i, 0)),
            pl.BlockSpec((tm, D_MODEL), lambda i: (i, gcol)),
            pl.BlockSpec((tm, D_MODEL), lambda i: (i, gcol + 1)),
            pl.BlockSpec((2, D_MODEL), const),
            pl.BlockSpec((ATTN_WIDTH, D_MODEL), const),
            pl.BlockSpec((D_INNER, D_MODEL), const),
            pl.BlockSpec((D_MODEL, D_MODEL), const),
            pl.BlockSpec((1, D_MODEL), const),
            pl.BlockSpec((2 * D_MODEL, LANES), const),
            pl.BlockSpec((1, LANES), const),
        ],
        out_specs=[
            pl.BlockSpec((tm, D_MODEL), lambda i: (i, 0)),
            pl.BlockSpec((tm * ROW_TILE, LANES), lambda i: (i, 0)),
            pl.BlockSpec((tm, LANES), lambda i: (i, 0)),
            pl.BlockSpec((tm, LANES), lambda i: (i, 0)),
            pl.BlockSpec((SUBLANES, LANES), const),
        ],
        out_shape=[
            jax.ShapeDtypeStruct((t, D_MODEL), F32),
            jax.ShapeDtypeStruct((t * ROW_TILE, LANES), F32),
            jax.ShapeDtypeStruct((t, LANES), jnp.int32),
            jax.ShapeDtypeStruct((t, LANES), F32),
            jax.ShapeDtypeStruct((SUBLANES, LANES), F32),
        ],
        scratch_shapes=[pltpu.VMEM((1, LANES), F32)],
        compiler_params=pltpu.CompilerParams(
            dimension_semantics=("arbitrary",), vmem_limit_bytes=VMEM_LIMIT),
        name="merge",
    )(x2, attn2, ssd2, proj, proj, gate_b, woa, wos, wout, ln2, wr, br)


GU_PAIR = 2 * LANES
N_GU = 2 * D_FF // GU_PAIR
GATHER_DEPTH = 2
GATHER_BUFS = GATHER_DEPTH + 1


def _expert_kernel(be_ref, tok_ref, h_hbm, wgu_ref, bgu_ref, wd_ref, bd_ref, y_ref, *scratch):
    i = pl.program_id(0)
    nblk = be_ref[pl.num_programs(0)]
    bufs, sem, wgu_s, wd_s = scratch[:GATHER_BUFS], *scratch[GATHER_BUFS:]

    @pl.when(((i == 0) | (be_ref[i] != be_ref[jnp.maximum(i - 1, 0)])) & (i < nblk))
    def _():
        r = lax.broadcasted_iota(jnp.int32, (GU_PAIR, GU_PAIR), 0)
        c = lax.broadcasted_iota(jnp.int32, (GU_PAIR, GU_PAIR), 1)
        perm = (r == jnp.where(c < LANES, 2 * c, 2 * (c - LANES) + 1)).astype(BF16)
        for k in range(N_GU):
            w = wgu_ref[0, :, k * GU_PAIR:(k + 1) * GU_PAIR].astype(BF16)
            wgu_s[k] = _dot(w, perm).astype(BF16)
        wd_s[...] = wd_ref[0].astype(BF16)

    def row_copy(blk, r, par):
        t = tok_ref[blk * MOE_BLOCK + r]
        src = h_hbm.at[pl.ds(pl.multiple_of(t * ROW_TILE, ROW_TILE), ROW_TILE), :]
        dst = bufs[par].at[pl.ds(pl.multiple_of(r * ROW_TILE, ROW_TILE), ROW_TILE), :]
        return pltpu.make_async_copy(src, dst, sem.at[par])

    def wait(blk, par):
        def body(r, carry):
            row_copy(blk, r, par).wait()
            return carry
        lax.fori_loop(0, MOE_BLOCK, body, 0, unroll=8)

    @pl.when(i == 0)
    def _():
        for d in range(GATHER_DEPTH):
            def body(r, carry, d=d):
                row_copy(jnp.minimum(d, nblk - 1), r, d).start()
                return carry
            lax.fori_loop(0, MOE_BLOCK, body, 0, unroll=8)

    @pl.when(i >= nblk)
    def _():
        y_ref[...] = jnp.zeros_like(y_ref)

    def step(par):
        wait(i, par)
        nxt = jnp.minimum(i + GATHER_DEPTH, nblk - 1)
        npar = (par + GATHER_DEPTH) % GATHER_BUFS
        for r in range(MOE_BLOCK):
            row_copy(nxt, r, npar).start(priority=r % 2)

        x = _load_row_tiles(bufs[par], MOE_BLOCK).astype(BF16)
        acts = []
        for k in range(N_GU):
            gu = _dot(x, wgu_s[k]) + bgu_ref[0, k]
            g = jnp.minimum(gu[:, :LANES], SWIGLU_LIMIT)
            u = jnp.clip(gu[:, LANES:], -SWIGLU_LIMIT, SWIGLU_LIMIT)
            acts.append(((u + 1.0) * (g * jax.nn.sigmoid(SWIGLU_ALPHA * g))).astype(BF16))
        _store_row_tiles(y_ref, _dot(jnp.concatenate(acts, axis=1), wd_s[...]) + bd_ref[0])

        @pl.when(i == nblk - 1)
        def _():
            for d in range(1, GATHER_BUFS):
                wait(nxt, (par + d) % GATHER_BUFS)

    for par in range(GATHER_BUFS):
        pl.when((lax.rem(i, GATHER_BUFS) == par) & (i < nblk))(functools.partial(step, par))


def _experts(block_expert, buf_tok, h2, wgu, bgu, wd, bd):
    n_rows = buf_tok.shape[0]
    n_blocks = n_rows // MOE_BLOCK
    wmap = lambda i, be, tok: (be[i], 0, 0)
    return pl.pallas_call(
        _expert_kernel,
        grid_spec=pltpu.PrefetchScalarGridSpec(
            num_scalar_prefetch=2,
            grid=(n_blocks,),
            in_specs=[
                pl.BlockSpec(memory_space=pl.ANY),
                pl.BlockSpec((1, D_MODEL, 2 * D_FF), wmap),
                pl.BlockSpec((1, N_GU, 1, GU_PAIR), lambda i, be, tok: (be[i], 0, 0, 0)),
                pl.BlockSpec((1, D_FF, D_MODEL), wmap),
                pl.BlockSpec((1, 1, D_MODEL), wmap),
            ],
            out_specs=pl.BlockSpec((MOE_BLOCK * ROW_TILE, LANES), lambda i, be, tok: (i, 0)),
            scratch_shapes=(
                [pltpu.VMEM((MOE_BLOCK * ROW_TILE, LANES), F32)] * GATHER_BUFS
                + [pltpu.SemaphoreType.DMA((GATHER_BUFS,)),
                   pltpu.VMEM((N_GU, D_MODEL, GU_PAIR), BF16),
                   pltpu.VMEM((D_FF, D_MODEL), BF16)]),
        ),
        out_shape=jax.ShapeDtypeStruct((n_rows * ROW_TILE, LANES), F32),
        compiler_params=pltpu.CompilerParams(
            dimension_semantics=("arbitrary",), vmem_limit_bytes=VMEM_LIMIT),
        name="experts",
    )(block_expert, buf_tok, h2, wgu, bgu, wd, bd)


COMBINE_TM = 128


def _combine_kernel(dest_ref, x1_ref, tw_ref, y_hbm, o_ref, *scratch):
    i = pl.program_id(0)
    nt = pl.num_programs(0)
    tm = COMBINE_TM
    bufs, sem = scratch[:GATHER_BUFS], scratch[GATHER_BUFS]

    def row_copy(tile, r, k, par):
        d = dest_ref[(tile * tm + r) * TOP_K + k]
        src = y_hbm.at[pl.ds(pl.multiple_of(d * ROW_TILE, ROW_TILE), ROW_TILE), :]
        dst = bufs[par].at[k, pl.ds(pl.multiple_of(r * ROW_TILE, ROW_TILE), ROW_TILE), :]
        return pltpu.make_async_copy(src, dst, sem.at[par])

    def wait(tile, par):
        def body(r, carry):
            for k in range(TOP_K):
                row_copy(tile, r, k, par).wait()
            return carry
        lax.fori_loop(0, tm, body, 0, unroll=2)

    @pl.when(i == 0)
    def _():
        for d in range(GATHER_DEPTH):
            def body(r, carry, d=d):
                for k in range(TOP_K):
                    row_copy(jnp.minimum(d, nt - 1), r, k, d).start()
                return carry
            lax.fori_loop(0, tm, body, 0, unroll=2)

    def step(par):
        wait(i, par)
        nxt = jnp.minimum(i + GATHER_DEPTH, nt - 1)
        npar = (par + GATHER_DEPTH) % GATHER_BUFS
        for r in range(tm):
            for k in range(TOP_K):
                row_copy(nxt, r, k, npar).start(priority=k % 2)

        tw = tw_ref[...]
        gate = [jnp.broadcast_to(tw[:, k:k + 1], (tm, LANES)) for k in range(TOP_K)]
        x1 = x1_ref[...]
        cols = []
        for c in range(ROW_TILE):
            acc = x1[:, c * LANES:(c + 1) * LANES]
            for k in range(TOP_K):
                acc = acc + bufs[par][k, pl.ds(c, tm, stride=ROW_TILE), :] * gate[k]
            cols.append(acc)
        o_ref[...] = jnp.concatenate(cols, axis=1)

        @pl.when(i == nt - 1)
        def _():
            for d in range(1, GATHER_BUFS):
                wait(nxt, (par + d) % GATHER_BUFS)

    for par in range(GATHER_BUFS):
        pl.when(lax.rem(i, GATHER_BUFS) == par)(functools.partial(step, par))


def _combine(dest_flat, x1, tw, yb):
    t = x1.shape[0]
    tm = COMBINE_TM
    return pl.pallas_call(
        _combine_kernel,
        grid_spec=pltpu.PrefetchScalarGridSpec(
            num_scalar_prefetch=1,
            grid=(t // tm,),
            in_specs=[
                pl.BlockSpec((tm, D_MODEL), lambda i, d: (i, 0)),
                pl.BlockSpec((tm, LANES), lambda i, d: (i, 0)),
                pl.BlockSpec(memory_space=pl.ANY),
            ],
            out_specs=pl.BlockSpec((tm, D_MODEL), lambda i, d: (i, 0)),
            scratch_shapes=(
                [pltpu.VMEM((TOP_K, tm * ROW_TILE, LANES), F32)] * GATHER_BUFS
                + [pltpu.SemaphoreType.DMA((GATHER_BUFS,))]),
        ),
        out_shape=jax.ShapeDtypeStruct((t, D_MODEL), F32),
        compiler_params=pltpu.CompilerParams(
            dimension_semantics=("arbitrary",), vmem_limit_bytes=VMEM_LIMIT),
        name="combine",
    )(dest_flat, x1, tw, yb)


ROWTOK_STEPS = 64


def _rowtok_kernel(dest_ref, pad_ref, tok_ref):
    j = pl.program_id(0)
    asg_per = dest_ref.shape[0] // ROWTOK_STEPS
    n_ranges = pad_ref.shape[0] // 2

    @pl.when(j == 0)
    def _():
        for e in range(n_ranges):
            def zero(r, carry):
                tok_ref[r] = 0
                return carry
            lax.fori_loop(pad_ref[e], pad_ref[n_ranges + e], zero, 0)

    base = j * asg_per

    def put(a, carry):
        a = base + a
        tok_ref[dest_ref[a]] = lax.shift_right_logical(a, TOP_K.bit_length() - 1)
        return carry
    lax.fori_loop(0, asg_per, put, 0, unroll=16)


def _row_tokens(dest_flat, pad_bounds, n_rows):
    assert TOP_K & (TOP_K - 1) == 0 and dest_flat.shape[0] % ROWTOK_STEPS == 0
    return pl.pallas_call(
        _rowtok_kernel,
        grid=(ROWTOK_STEPS,),
        in_specs=[pl.BlockSpec(memory_space=pltpu.SMEM), pl.BlockSpec(memory_space=pltpu.SMEM)],
        out_specs=pl.BlockSpec(memory_space=pltpu.SMEM),
        out_shape=jax.ShapeDtypeStruct((n_rows,), jnp.int32),
        compiler_params=pltpu.CompilerParams(dimension_semantics=("arbitrary",)),
        name="rowtok",
    )(dest_flat, pad_bounds)


def _routing_tables(top_idx, pos, counts, t):
    padded = ((counts + MOE_BLOCK - 1) // MOE_BLOCK) * MOE_BLOCK
    cum_padded = jnp.cumsum(padded)
    pstart = cum_padded - padded
    dest = (pstart[top_idx] + pos).astype(jnp.int32)
    n_rows = t * TOP_K + N_EXPERTS * MOE_BLOCK
    n_blocks = n_rows // MOE_BLOCK
    tail = jnp.full((1,), n_rows, jnp.int32)
    pad_bounds = jnp.concatenate([pstart + counts, cum_padded[-1:], cum_padded, tail]).astype(jnp.int32)
    buf_tok = _row_tokens(dest.reshape(-1), pad_bounds, n_rows)
    block_start = jnp.arange(n_blocks, dtype=jnp.int32) * MOE_BLOCK
    block_expert = jnp.minimum(
        jnp.sum((cum_padded[None, :] <= block_start[:, None]).astype(jnp.int32), axis=1), N_EXPERTS - 1)
    block_expert = jnp.concatenate([block_expert, cum_padded[-1:] // MOE_BLOCK]).astype(jnp.int32)
    return dest, buf_tok, block_expert


def kernel(x, ln1_w, w_in, gate_b, q_norm_w, k_norm_w, conv_w, conv_b, dt_bias, a_log, d_skip, ssd_norm_w,
           w_o_attn, w_o_ssd, w_out, ln2_w, w_router, b_router, w_gate_up, b_gate_up, w_down, b_down):
    b, seq, d = x.shape
    t = b * seq
    assert d == D_MODEL and seq % MOBA_BLOCK == 0 and seq % SSD_CHUNK == 0
    assert ln1_w.shape[0] == 1, "single layer"
    x2 = x.reshape(t, d)

    wi = w_in[0]
    col_dt = 3 * ATTN_WIDTH + D_INNER + D_XBC
    w_r = jnp.concatenate(
        [wi[:, :col_dt], wi[:, col_dt + SSD_HEADS:], wi[:, col_dt:col_dt + SSD_HEADS],
         jnp.zeros((d, LANES - SSD_HEADS), wi.dtype)], axis=1).astype(BF16)
    qw2 = jnp.tile(q_norm_w[0], 2)[None, :]
    kw2 = jnp.tile(k_norm_w[0], 2)[None, :]
    pad_g = lambda v: jnp.pad(v.reshape(N_GROUPS, 1, HEADS_PER_GROUP),
                              ((0, 0), (0, 0), (0, LANES - HEADS_PER_GROUP)))
    dtb_g = pad_g(dt_bias[0])
    alog_g = pad_g(a_log[0])
    dskip_e = jnp.repeat(d_skip[0], SSD_HEAD_DIM).reshape(N_GROUPS, 1, GROUP_WIDTH)
    nw_g = ssd_norm_w[0].reshape(N_GROUPS, 1, GROUP_WIDTH)
    wr_hi = w_router[0].astype(BF16)
    wr_lo = (w_router[0] - wr_hi.astype(F32)).astype(BF16)
    zr = jnp.zeros((d, N_EXPERTS), BF16)
    wr_p = jnp.concatenate([jnp.concatenate([wr_hi, wr_lo, zr, zr], axis=1),
                            jnp.concatenate([wr_hi, zr, zr, zr], axis=1)], axis=0)
    br_p = jnp.pad(b_router[0], (0, LANES - N_EXPERTS))[None, :]
    bgu = b_gate_up[0].reshape(N_EXPERTS, 2 * D_FF // GU_PAIR, LANES, 2).transpose(0, 1, 3, 2)
    bgu = bgu.reshape(N_EXPERTS, N_GU, 1, GU_PAIR)
    bd = b_down[0][:, None, :]

    proj = _inproj(x2, ln1_w, w_r)
    proj3 = proj.reshape(b, seq, NP)
    attn = _moba(proj3, qw2, kw2)
    ssd = _ssd(proj3, conv_w[0], conv_b, dtb_g, alog_g, dskip_e, nw_g)
    x1, h2, ti, tw, cnt = _merge(x2, attn.reshape(t, ATTN_WIDTH), ssd.reshape(t, D_INNER), proj, gate_b[0],
                                 w_o_attn[0].astype(BF16), w_o_ssd[0].astype(BF16), w_out[0].astype(BF16),
                                 ln2_w, wr_p, br_p)

    dest, buf_tok, block_expert = _routing_tables(
        ti[:, :TOP_K], ti[:, TOP_K:2 * TOP_K], cnt[0, :N_EXPERTS].astype(jnp.int32), t)
    yb = _experts(block_expert, buf_tok, h2, w_gate_up[0], bgu, w_down[0], bd)
    out = _combine(dest.reshape(-1), x1, tw, yb)
    return out.reshape(b, seq, d)
```

```python
import functools

import jax
import jax.numpy as jnp
from jax import lax
from jax.experimental import pallas as pl
from jax.experimental.pallas import tpu as pltpu

F32 = jnp.float32
BF16 = jnp.bfloat16

D_MODEL = 1024
N_HEADS = 16
HEAD_DIM = 64
ATTN_WIDTH = N_HEADS * HEAD_DIM
MOBA_BLOCK = 256
MOBA_TOPK = 3
D_INNER = 2048
SSD_HEAD_DIM = 64
SSD_HEADS = D_INNER // SSD_HEAD_DIM
N_GROUPS = 4
HEADS_PER_GROUP = SSD_HEADS // N_GROUPS
GROUP_WIDTH = HEADS_PER_GROUP * SSD_HEAD_DIM
D_STATE = 128
CONV_K = 4
SSD_CHUNK = 128
D_XBC = D_INNER + 2 * N_GROUPS * D_STATE
N_EXPERTS = 32
TOP_K = 4
D_FF = D_MODEL
SWIGLU_LIMIT = 7.0
SWIGLU_ALPHA = 1.702
MOE_BLOCK = 256
NORM_EPS = 1e-6
NEG_INF = -1e30

LANES = 128
SUBLANES = 8

P_Q = 0
P_Z = 3 * ATTN_WIDTH
P_XBC = P_Z + D_INNER
P_GATE = P_XBC + D_XBC
P_DT = P_GATE + 2 * D_MODEL
NP = P_DT + LANES
PROJ_TN = 1152
VMEM_LIMIT = 56 * 1024 * 1024


def _split3(v):
    hi = v.astype(BF16)
    r1 = v - hi.astype(F32)
    mid = r1.astype(BF16)
    lo = (r1 - mid.astype(F32)).astype(BF16)
    return hi, mid, lo


def _dot(a, b):
    return jnp.dot(a, b, preferred_element_type=F32)


def _dot_nt(a, b):
    return lax.dot_general(a, b, (((1,), (1,)), ((), ())), preferred_element_type=F32)


ROW_TILE = D_MODEL // LANES


def _store_row_tiles(ref, val):
    rows = val.shape[0]
    for c in range(ROW_TILE):
        ref[pl.ds(c, rows, stride=ROW_TILE), :] = val[:, c * LANES:(c + 1) * LANES]


def _load_row_tiles(ref, rows):
    return jnp.concatenate([ref[pl.ds(c, rows, stride=ROW_TILE), :] for c in range(ROW_TILE)], axis=1)


def _inproj_kernel(x_ref, lnw_ref, w_ref, o_ref, h_ref):
    @pl.when(pl.program_id(1) == 0)
    def _():
        x = x_ref[...]
        ms = jnp.mean(x * x, axis=-1, keepdims=True)
        h_ref[...] = (x * lax.rsqrt(ms + NORM_EPS) * lnw_ref[...]).astype(BF16)

    o_ref[...] = _dot(h_ref[...], w_ref[...]).astype(o_ref.dtype)


def _inproj(x2, lnw, w_r):
    t = x2.shape[0]
    tm = min(1024, t)
    return pl.pallas_call(
        _inproj_kernel,
        grid=(t // tm, NP // PROJ_TN),
        in_specs=[
            pl.BlockSpec((tm, D_MODEL), lambda i, j: (i, 0)),
            pl.BlockSpec((1, D_MODEL), lambda i, j: (0, 0)),
            pl.BlockSpec((D_MODEL, PROJ_TN), lambda i, j: (0, j)),
        ],
        out_specs=pl.BlockSpec((tm, PROJ_TN), lambda i, j: (i, j)),
        out_shape=jax.ShapeDtypeStruct((t, NP), BF16),
        scratch_shapes=[pltpu.VMEM((tm, D_MODEL), BF16)],
        compiler_params=pltpu.CompilerParams(
            dimension_semantics=("parallel", "arbitrary"), vmem_limit_bytes=VMEM_LIMIT),
        name="inproj",
    )(x2, lnw, w_r)


def _moba_kernel(q_ref, k_ref, v_ref, qw_ref, kw_ref, o_ref, *, seq, nb):
    lane = lax.broadcasted_iota(jnp.int32, (1, LANES), 1)
    head0 = lane < HEAD_DIM
    r = lax.broadcasted_iota(jnp.int32, (2 * LANES, LANES), 0) % LANES // HEAD_DIM
    c = lax.broadcasted_iota(jnp.int32, (2 * LANES, LANES), 1) // HEAD_DIM
    avg2 = jnp.where(r == c, 1.0 / HEAD_DIM, 0.0).astype(BF16)

    def qk_norm(t_ref, w_ref):
        t = t_ref[0].astype(F32)
        sq = t * t
        hi = sq.astype(BF16)
        lo = (sq - hi.astype(F32)).astype(BF16)
        ms = _dot(jnp.concatenate([hi, lo], axis=1), avg2)
        return t * lax.rsqrt(ms + NORM_EPS) * w_ref[...]

    qn = qk_norm(q_ref, qw_ref)
    kn = qk_norm(k_ref, kw_ref)
    qs = qn * (HEAD_DIM ** -0.5)
    qs_b = qs.astype(BF16)

    kmean = jnp.mean(kn.reshape(nb, MOBA_BLOCK, LANES), axis=1)
    kmx = jnp.concatenate([jnp.where(head0, kmean, 0.0), jnp.where(head0, 0.0, kmean)], axis=0)
    kmx_hi = kmx.astype(BF16)
    kmx_lo = (kmx - kmx_hi.astype(F32)).astype(BF16)
    st = _dot_nt(jnp.concatenate([kmx_hi, kmx_lo], axis=0), qs_b)
    st = st[:2 * nb] + st[2 * nb:]

    qblk = lax.broadcasted_iota(jnp.int32, (nb, seq), 1) // MOBA_BLOCK
    jrow = lax.broadcasted_iota(jnp.int32, (nb, seq), 0)
    past = jrow < qblk
    bias = []
    for a in range(2):
        sm = jnp.where(past, st[a * nb:(a + 1) * nb], NEG_INF)
        rank = jnp.zeros((nb, seq), jnp.int32)
        for jp in range(nb):
            other = sm[jp:jp + 1, :]
            ahead = (other > sm) | ((other == sm) & (jp < jrow))
            rank = rank + ahead.astype(jnp.int32)
        bias.append(jnp.where(past & (rank >= MOBA_TOPK), NEG_INF, 0.0).astype(F32))
    zpad = jnp.zeros((HEAD_DIM - nb, seq), F32)
    bias_t = jnp.concatenate([bias[1], zpad, bias[0], zpad], axis=0).T

    rblk = lax.broadcasted_iota(jnp.int32, (seq, LANES), 0) // MOBA_BLOCK
    l64 = lax.broadcasted_iota(jnp.int32, (seq, LANES), 1) % HEAD_DIM
    ind = (l64 == rblk).astype(F32)

    q_aug = (jnp.where(head0, qs, bias_t).astype(BF16), jnp.where(head0, bias_t, qs).astype(BF16))
    k_aug = (jnp.where(head0, kn, ind).astype(BF16), jnp.where(head0, ind, kn).astype(BF16))
    v = v_ref[0]
    one = jnp.ones((), BF16)
    v_aug = (jnp.where(head0, v, one), jnp.where(head0, one, v))

    tri = (lax.broadcasted_iota(jnp.int32, (MOBA_BLOCK, MOBA_BLOCK), 0)
           >= lax.broadcasted_iota(jnp.int32, (MOBA_BLOCK, MOBA_BLOCK), 1))
    for i in range(nb):
        lo, hi = i * MOBA_BLOCK, (i + 1) * MOBA_BLOCK
        outs = []
        for a in range(2):
            s = _dot_nt(q_aug[a][lo:hi], k_aug[a][:hi])
            own = jnp.where(tri, s[:, lo:], NEG_INF)
            s = own if i == 0 else jnp.concatenate([s[:, :lo], own], axis=1)
            m = jnp.max(s, axis=1, keepdims=True)
            p = jnp.exp(s - m).astype(BF16)
            o = _dot(p, v_aug[a][:hi])
            outs.append(o / pltpu.roll(o, HEAD_DIM, axis=1))
        o_ref[0, lo:hi, :] = jnp.where(head0, outs[0], outs[1]).astype(o_ref.dtype)


def _moba(proj3, qw2, kw2):
    b, seq, _ = proj3.shape
    nb = seq // MOBA_BLOCK
    npair = ATTN_WIDTH // LANES
    return pl.pallas_call(
        functools.partial(_moba_kernel, seq=seq, nb=nb),
        grid=(b, npair),
        in_specs=[
            pl.BlockSpec((1, seq, LANES), lambda i, j: (i, 0, j)),
            pl.BlockSpec((1, seq, LANES), lambda i, j: (i, 0, npair + j)),
            pl.BlockSpec((1, seq, LANES), lambda i, j: (i, 0, 2 * npair + j)),
            pl.BlockSpec((1, LANES), lambda i, j: (0, 0)),
            pl.BlockSpec((1, LANES), lambda i, j: (0, 0)),
        ],
        out_specs=pl.BlockSpec((1, seq, LANES), lambda i, j: (i, 0, j)),
        out_shape=jax.ShapeDtypeStruct((b, seq, ATTN_WIDTH), BF16),
        compiler_params=pltpu.CompilerParams(
            dimension_semantics=("parallel", "parallel"), vmem_limit_bytes=VMEM_LIMIT),
        name="moba",
    )(proj3, proj3, proj3, qw2, kw2)


def _ssd_kernel(x_ref, b_ref, c_ref, z_ref, dt_ref, cwx_ref, cwb_ref, cwc_ref, cbx_ref, cbb_ref, cbc_ref,
                dtb_ref, alog_ref, dskip_ref, nw_ref, o_ref, st_ref, *, nc):
    g = pl.program_id(1)
    ch = SSD_CHUNK
    st_ref[...] = jnp.zeros_like(st_ref)

    ri = lax.broadcasted_iota(jnp.int32, (LANES, LANES), 0)
    ci = lax.broadcasted_iota(jnp.int32, (LANES, LANES), 1)
    sel_g = ((ri == g * HEADS_PER_GROUP + ci) & (ci < HEADS_PER_GROUP)).astype(BF16)
    neg_a = jnp.where(ci[0:1, :] < HEADS_PER_GROUP, -jnp.exp(alog_ref[0]), 0.0)
    tril_b = ri >= ci
    tril = tril_b.astype(BF16)
    tril3 = jnp.concatenate([tril, tril, tril], axis=1)
    er = lax.broadcasted_iota(jnp.int32, (LANES, GROUP_WIDTH), 0)
    ec = lax.broadcasted_iota(jnp.int32, (LANES, GROUP_WIDTH), 1) // SSD_HEAD_DIM
    e1 = (er == ec).astype(BF16)
    e3 = jnp.concatenate([e1, e1, e1], axis=0)
    lane_h0 = lax.broadcasted_iota(jnp.int32, (1, LANES), 1) < SSD_HEAD_DIM
    pack = 2 * SUBLANES

    def chunk(c, carry):
        r0 = pl.multiple_of(c * ch, ch)
        p0 = pl.multiple_of(jnp.maximum(r0 - pack, 0), pack)
        has_prev = c > 0

        def conv_silu(ref, w_ref, bias_ref):
            cur = ref[0, pl.ds(r0, ch), :].astype(F32)
            prev = ref[0, pl.ds(p0, pack), :].astype(F32)[SUBLANES:]
            prev = jnp.where(has_prev, prev, 0.0)
            cat = jnp.concatenate([prev, cur], axis=0)
            w = w_ref[...]
            out = cur * w[CONV_K - 1:CONV_K, :]
            for k in range(1, CONV_K):
                out = out + pltpu.roll(cat, k, axis=0)[SUBLANES:] * w[CONV_K - 1 - k:CONV_K - k, :]
            out = out + bias_ref[...]
            return out * jax.nn.sigmoid(out)

        xs = conv_silu(x_ref, cwx_ref, cbx_ref)
        bm = conv_silu(b_ref, cwb_ref, cbb_ref)
        cm = conv_silu(c_ref, cwc_ref, cbc_ref)
        _ssd_chunk(xs, bm, cm, dt_ref[0, pl.ds(r0, ch), :], z_ref[0, pl.ds(r0, ch), :], sel_g, neg_a, tril_b,
                   tril3, e3, lane_h0, dtb_ref, dskip_ref, nw_ref, o_ref.at[0, pl.ds(r0, ch), :], st_ref)
        return carry

    lax.fori_loop(0, nc, chunk, 0, unroll=2)


def _ssd_chunk(xs, bm, cm, dt_raw, z_raw, sel_g, neg_a, tril_b, tril3, e3, lane_h0, dtb_ref, dskip_ref, nw_ref,
               o_ref, st_ref):
    ch = SSD_CHUNK
    dtr = _dot(dt_raw, sel_g) + dtb_ref[0]
    dtf = jnp.maximum(dtr, 0.0) + jnp.log1p(jnp.exp(-jnp.abs(dtr)))
    a = dtf * neg_a

    a_cum = _dot(tril3, jnp.concatenate(_split3(a), axis=0))
    a_tot = a_cum[ch - 1:ch, :]

    vals = jnp.concatenate([dtf, jnp.exp(a_cum), jnp.exp(a_tot - a_cum)], axis=0)
    ex = _dot(jnp.concatenate(_split3(vals), axis=1), e3)
    dt_e, eac_e, dec_e = ex[:ch], ex[ch:2 * ch], ex[2 * ch:]
    etot_e = eac_e[ch - 1:ch, :]

    xdt = xs * dt_e
    xdt_b = xdt.astype(BF16)
    bm_b = bm.astype(BF16)
    cm_b = cm.astype(BF16)
    cb = _dot_nt(cm_b, bm_b)
    a_cum_t = a_cum.T
    pieces = []
    for hp in range(HEADS_PER_GROUP // 2):
        xpair = xdt_b[:, hp * LANES:(hp + 1) * LANES]
        yd = []
        for hh in range(2):
            h = 2 * hp + hh
            seg = a_cum[:, h:h + 1] - a_cum_t[h:h + 1, :]
            lmat = jnp.exp(jnp.where(tril_b, seg, -jnp.inf))
            yd.append(_dot((cb * lmat).astype(BF16), xpair))
        pieces.append(jnp.where(lane_h0, yd[0], yd[1]))
    y_diag = jnp.concatenate(pieces, axis=1)

    s_prev = st_ref[...]
    y_off = _dot(cm_b, s_prev.astype(BF16)) * eac_e
    st_ref[...] = etot_e * s_prev + _dot(bm.T.astype(BF16), (xdt * dec_e).astype(BF16))

    y = y_diag + y_off + dskip_ref[0] * xs
    z = z_raw.astype(F32)
    gated = y * (z * jax.nn.sigmoid(z))
    ms = jnp.mean(gated * gated, axis=-1, keepdims=True)
    o_ref[...] = (gated * lax.rsqrt(ms + NORM_EPS) * nw_ref[0]).astype(o_ref.dtype)


def _ssd(proj3, conv_w, conv_b2, dtb_g, alog_g, dskip_e, nw_g):
    b, seq, _ = proj3.shape
    nc = seq // SSD_CHUNK
    gw = GROUP_WIDTH
    xo, bo, co, zo = P_XBC // gw, (P_XBC + D_INNER) // LANES, (P_XBC + D_INNER) // LANES + N_GROUPS, P_Z // gw
    return pl.pallas_call(
        functools.partial(_ssd_kernel, nc=nc),
        grid=(b, N_GROUPS),
        in_specs=[
            pl.BlockSpec((1, seq, gw), lambda i, g: (i, 0, xo + g)),
            pl.BlockSpec((1, seq, LANES), lambda i, g: (i, 0, bo + g)),
            pl.BlockSpec((1, seq, LANES), lambda i, g: (i, 0, co + g)),
            pl.BlockSpec((1, seq, gw), lambda i, g: (i, 0, zo + g)),
            pl.BlockSpec((1, seq, LANES), lambda i, g: (i, 0, P_DT // LANES)),
            pl.BlockSpec((CONV_K, gw), lambda i, g: (0, g)),
            pl.BlockSpec((CONV_K, LANES), lambda i, g: (0, D_INNER // LANES + g)),
            pl.BlockSpec((CONV_K, LANES), lambda i, g: (0, D_INNER // LANES + N_GROUPS + g)),
            pl.BlockSpec((1, gw), lambda i, g: (0, g)),
            pl.BlockSpec((1, LANES), lambda i, g: (0, D_INNER // LANES + g)),
            pl.BlockSpec((1, LANES), lambda i, g: (0, D_INNER // LANES + N_GROUPS + g)),
            pl.BlockSpec((1, 1, LANES), lambda i, g: (g, 0, 0)),
            pl.BlockSpec((1, 1, LANES), lambda i, g: (g, 0, 0)),
            pl.BlockSpec((1, 1, gw), lambda i, g: (g, 0, 0)),
            pl.BlockSpec((1, 1, gw), lambda i, g: (g, 0, 0)),
        ],
        out_specs=pl.BlockSpec((1, seq, gw), lambda i, g: (i, 0, g)),
        out_shape=jax.ShapeDtypeStruct((b, seq, D_INNER), BF16),
        scratch_shapes=[pltpu.VMEM((D_STATE, gw), F32)],
        compiler_params=pltpu.CompilerParams(
            dimension_semantics=("parallel", "parallel"), vmem_limit_bytes=VMEM_LIMIT),
        name="ssd",
    )(proj3, proj3, proj3, proj3, proj3, conv_w, conv_w, conv_w, conv_b2, conv_b2, conv_b2,
      dtb_g, alog_g, dskip_e, nw_g)


def _merge_kernel(x_ref, attn_ref, ssd_ref, gl0_ref, gl1_ref, gb_ref, woa_ref, wos_ref, wout_ref,
                  ln2_ref, wr_ref, br_ref, x1_ref, h2_ref, ti_ref, tw_ref, cnt_ref, run_ref):
    ao = _dot(attn_ref[...], woa_ref[...])
    so = _dot(ssd_ref[...], wos_ref[...])
    g0 = jax.nn.sigmoid(gl0_ref[...].astype(F32) + gb_ref[0:1, :])
    g1 = jax.nn.sigmoid(gl1_ref[...].astype(F32) + gb_ref[1:2, :])
    mixed = (g0 * ao + g1 * so).astype(BF16)
    x1 = x_ref[...] + _dot(mixed, wout_ref[...])
    x1_ref[...] = x1
    ms = jnp.mean(x1 * x1, axis=-1, keepdims=True)
    h2 = x1 * lax.rsqrt(ms + NORM_EPS) * ln2_ref[...]
    _store_row_tiles(h2_ref, h2)

    hh = h2.astype(BF16)
    hl = (h2 - hh.astype(F32)).astype(BF16)
    r = _dot(jnp.concatenate([hh, hl], axis=1), wr_ref[...])
    logits = r + pltpu.roll(r, LANES - N_EXPERTS, axis=1) + br_ref[...]
    lane = lax.broadcasted_iota(jnp.int32, logits.shape, 1)
    lane_f = lane.astype(F32)
    cur = jnp.where(lane < N_EXPERTS, logits, -jnp.inf)
    vals, idxs = [], []
    for _ in range(TOP_K):
        m = jnp.max(cur, axis=1, keepdims=True)
        idx = jnp.min(jnp.where(cur == m, lane_f, float(LANES)), axis=1,
                      keepdims=True).astype(jnp.int32)
        vals.append(m)
        idxs.append(idx)
        cur = jnp.where(lane == idx, -jnp.inf, cur)
    es = [jnp.exp(v - vals[0]) for v in vals]
    den = es[0] + es[1] + es[2] + es[3]

    @pl.when(pl.program_id(0) == 0)
    def _():
        run_ref[...] = jnp.zeros_like(run_ref)

    tm = logits.shape[0]
    multi_hot = jnp.zeros(logits.shape, F32)
    for k in range(TOP_K):
        multi_hot = multi_hot + (lane == idxs[k]).astype(F32)
    earlier = (lax.broadcasted_iota(jnp.int32, (tm, tm), 0)
               > lax.broadcasted_iota(jnp.int32, (tm, tm), 1)).astype(BF16)
    before = run_ref[...] + _dot(earlier, multi_hot.astype(BF16))
    run = run_ref[...] + jnp.sum(multi_hot, axis=0, keepdims=True)
    run_ref[...] = run
    cnt_ref[...] = jnp.broadcast_to(run, cnt_ref.shape)

    ti = jnp.zeros(logits.shape, jnp.int32)
    tw = jnp.zeros(logits.shape, F32)
    for k in range(TOP_K):
        pos = jnp.sum(jnp.where(lane == idxs[k], before, 0.0), axis=1, keepdims=True).astype(jnp.int32)
        ti = jnp.where(lane == k, idxs[k], ti)
        ti = jnp.where(lane == TOP_K + k, pos, ti)
        tw = jnp.where(lane == k, es[k] / den, tw)
    ti_ref[...] = ti
    tw_ref[...] = tw


def _merge(x2, attn2, ssd2, proj, gate_b, woa, wos, wout, ln2, wr, br):
    t = x2.shape[0]
    tm = min(512, t)
    const = lambda i: (0, 0)
    gcol = P_GATE // D_MODEL
    return pl.pallas_call(
        _merge_kernel,
        grid=(t // tm,),
        in_specs=[
            pl.BlockSpec((tm, D_MODEL), lambda i: (i, 0)),
            pl.BlockSpec((tm, ATTN_WIDTH), lambda i: (i, 0)),
            pl.BlockSpec((tm, D_INNER), lambda i: (i, 0)),
            pl.BlockSpec((tm, D_MODEL), lambda i: (i, gcol)),
            pl.BlockSpec((tm, D_MODEL), lambda i: (i, gcol + 1)),
            pl.BlockSpec((2, D_MODEL), const),
            pl.BlockSpec((ATTN_WIDTH, D_MODEL), const),
            pl.BlockSpec((D_INNER, D_MODEL), const),
            pl.BlockSpec((D_MODEL, D_MODEL), const),
            pl.BlockSpec((1, D_MODEL), const),
            pl.BlockSpec((2 * D_MODEL, LANES), const),
            pl.BlockSpec((1, LANES), const),
        ],
        out_specs=[
            pl.BlockSpec((tm, D_MODEL), lambda i: (i, 0)),
            pl.BlockSpec((tm * ROW_TILE, LANES), lambda i: (i, 0)),
            pl.BlockSpec((tm, LANES), lambda i: (i, 0)),
            pl.BlockSpec((tm, LANES), lambda i: (i, 0)),
            pl.BlockSpec((SUBLANES, LANES), const),
        ],
        out_shape=[
            jax.ShapeDtypeStruct((t, D_MODEL), F32),
            jax.ShapeDtypeStruct((t * ROW_TILE, LANES), F32),
            jax.ShapeDtypeStruct((t, LANES), jnp.int32),
            jax.ShapeDtypeStruct((t, LANES), F32),
            jax.ShapeDtypeStruct((SUBLANES, LANES), F32),
        ],
        scratch_shapes=[pltpu.VMEM((1, LANES), F32)],
        compiler_params=pltpu.CompilerParams(
            dimension_semantics=("arbitrary",), vmem_limit_bytes=VMEM_LIMIT),
        name="merge",
    )(x2, attn2, ssd2, proj, proj, gate_b, woa, wos, wout, ln2, wr, br)


GU_PAIR = 2 * LANES
N_GU = 2 * D_FF // GU_PAIR
GATHER_DEPTH = 2
GATHER_BUFS = GATHER_DEPTH + 1


def _expert_kernel(be_ref, tok_ref, h_hbm, wgu_ref, bgu_ref, wd_ref, bd_ref, y_ref, *scratch):
    i = pl.program_id(0)
    nblk = be_ref[pl.num_programs(0)]
    bufs, sem, wgu_s, wd_s = scratch[:GATHER_BUFS], *scratch[GATHER_BUFS:]

    @pl.when(((i == 0) | (be_ref[i] != be_ref[jnp.maximum(i - 1, 0)])) & (i < nblk))
    def _():
        r = lax.broadcasted_iota(jnp.int32, (GU_PAIR, GU_PAIR), 0)
        c = lax.broadcasted_iota(jnp.int32, (GU_PAIR, GU_PAIR), 1)
        perm = (r == jnp.where(c < LANES, 2 * c, 2 * (c - LANES) + 1)).astype(BF16)
        for k in range(N_GU):
            w = wgu_ref[0, :, k * GU_PAIR:(k + 1) * GU_PAIR].astype(BF16)
            wgu_s[k] = _dot(w, perm).astype(BF16)
        wd_s[...] = wd_ref[0].astype(BF16)

    def row_copy(blk, r, par):
        t = tok_ref[blk * MOE_BLOCK + r]
        src = h_hbm.at[pl.ds(pl.multiple_of(t * ROW_TILE, ROW_TILE), ROW_TILE), :]
        dst = bufs[par].at[pl.ds(pl.multiple_of(r * ROW_TILE, ROW_TILE), ROW_TILE), :]
        return pltpu.make_async_copy(src, dst, sem.at[par])

    def wait(blk, par):
        def body(r, carry):
            row_copy(blk, r, par).wait()
            return carry
        lax.fori_loop(0, MOE_BLOCK, body, 0, unroll=8)

    @pl.when(i == 0)
    def _():
        for d in range(GATHER_DEPTH):
            def body(r, carry, d=d):
                row_copy(jnp.minimum(d, nblk - 1), r, d).start()
                return carry
            lax.fori_loop(0, MOE_BLOCK, body, 0, unroll=8)

    @pl.when(i >= nblk)
    def _():
        y_ref[...] = jnp.zeros_like(y_ref)

    def step(par):
        wait(i, par)
        nxt = jnp.minimum(i + GATHER_DEPTH, nblk - 1)
        npar = (par + GATHER_DEPTH) % GATHER_BUFS
        for r in range(MOE_BLOCK):
            row_copy(nxt, r, npar).start(priority=r % 2)

        x = _load_row_tiles(bufs[par], MOE_BLOCK).astype(BF16)
        acts = []
        for k in range(N_GU):
            gu = _dot(x, wgu_s[k]) + bgu_ref[0, k]
            g = jnp.minimum(gu[:, :LANES], SWIGLU_LIMIT)
            u = jnp.clip(gu[:, LANES:], -SWIGLU_LIMIT, SWIGLU_LIMIT)
            acts.append(((u + 1.0) * (g * jax.nn.sigmoid(SWIGLU_ALPHA * g))).astype(BF16))
        _store_row_tiles(y_ref, _dot(jnp.concatenate(acts, axis=1), wd_s[...]) + bd_ref[0])

        @pl.when(i == nblk - 1)
        def _():
            for d in range(1, GATHER_BUFS):
                wait(nxt, (par + d) % GATHER_BUFS)

    for par in range(GATHER_BUFS):
        pl.when((lax.rem(i, GATHER_BUFS) == par) & (i < nblk))(functools.partial(step, par))


def _experts(block_expert, buf_tok, h2, wgu, bgu, wd, bd):
    n_rows = buf_tok.shape[0]
    n_blocks = n_rows // MOE_BLOCK
    wmap = lambda i, be, tok: (be[i], 0, 0)
    return pl.pallas_call(
        _expert_kernel,
        grid_spec=pltpu.PrefetchScalarGridSpec(
            num_scalar_prefetch=2,
            grid=(n_blocks,),
            in_specs=[
                pl.BlockSpec(memory_space=pl.ANY),
                pl.BlockSpec((1, D_MODEL, 2 * D_FF), wmap),
                pl.BlockSpec((1, N_GU, 1, GU_PAIR), lambda i, be, tok: (be[i], 0, 0, 0)),
                pl.BlockSpec((1, D_FF, D_MODEL), wmap),
                pl.BlockSpec((1, 1, D_MODEL), wmap),
            ],
            out_specs=pl.BlockSpec((MOE_BLOCK * ROW_TILE, LANES), lambda i, be, tok: (i, 0)),
            scratch_shapes=(
                [pltpu.VMEM((MOE_BLOCK * ROW_TILE, LANES), F32)] * GATHER_BUFS
                + [pltpu.SemaphoreType.DMA((GATHER_BUFS,)),
                   pltpu.VMEM((N_GU, D_MODEL, GU_PAIR), BF16),
                   pltpu.VMEM((D_FF, D_MODEL), BF16)]),
        ),
        out_shape=jax.ShapeDtypeStruct((n_rows * ROW_TILE, LANES), F32),
        compiler_params=pltpu.CompilerParams(
            dimension_semantics=("arbitrary",), vmem_limit_bytes=VMEM_LIMIT),
        name="experts",
    )(block_expert, buf_tok, h2, wgu, bgu, wd, bd)


COMBINE_TM = 256


def _combine_kernel(dest_ref, x1_ref, tw_ref, y_hbm, o_ref, *scratch):
    i = pl.program_id(0)
    nt = pl.num_programs(0)
    tm = COMBINE_TM
    bufs, sem = scratch[:GATHER_BUFS], scratch[GATHER_BUFS]

    def row_copy(tile, r, k, par):
        d = dest_ref[(tile * tm + r) * TOP_K + k]
        src = y_hbm.at[pl.ds(pl.multiple_of(d * ROW_TILE, ROW_TILE), ROW_TILE), :]
        dst = bufs[par].at[k, pl.ds(pl.multiple_of(r * ROW_TILE, ROW_TILE), ROW_TILE), :]
        return pltpu.make_async_copy(src, dst, sem.at[par])

    def wait(tile, par):
        def body(r, carry):
            for k in range(TOP_K):
                row_copy(tile, r, k, par).wait()
            return carry
        lax.fori_loop(0, tm, body, 0, unroll=2)

    @pl.when(i == 0)
    def _():
        for d in range(GATHER_DEPTH):
            def body(r, carry, d=d):
                for k in range(TOP_K):
                    row_copy(jnp.minimum(d, nt - 1), r, k, d).start()
                return carry
            lax.fori_loop(0, tm, body, 0, unroll=2)

    def step(par):
        wait(i, par)
        nxt = jnp.minimum(i + GATHER_DEPTH, nt - 1)
        npar = (par + GATHER_DEPTH) % GATHER_BUFS
        for r in range(tm):
            for k in range(TOP_K):
                row_copy(nxt, r, k, npar).start(priority=k % 2)

        tw = tw_ref[...]
        gate = [jnp.broadcast_to(tw[:, k:k + 1], (tm, LANES)) for k in range(TOP_K)]
        x1 = x1_ref[...]
        cols = []
        for c in range(ROW_TILE):
            acc = x1[:, c * LANES:(c + 1) * LANES]
            for k in range(TOP_K):
                acc = acc + bufs[par][k, pl.ds(c, tm, stride=ROW_TILE), :] * gate[k]
            cols.append(acc)
        o_ref[...] = jnp.concatenate(cols, axis=1)

        @pl.when(i == nt - 1)
        def _():
            for d in range(1, GATHER_BUFS):
                wait(nxt, (par + d) % GATHER_BUFS)

    for par in range(GATHER_BUFS):
        pl.when(lax.rem(i, GATHER_BUFS) == par)(functools.partial(step, par))


def _combine(dest_flat, x1, tw, yb):
    t = x1.shape[0]
    tm = COMBINE_TM
    return pl.pallas_call(
        _combine_kernel,
        grid_spec=pltpu.PrefetchScalarGridSpec(
            num_scalar_prefetch=1,
            grid=(t // tm,),
            in_specs=[
                pl.BlockSpec((tm, D_MODEL), lambda i, d: (i, 0)),
                pl.BlockSpec((tm, LANES), lambda i, d: (i, 0)),
                pl.BlockSpec(memory_space=pl.ANY),
            ],
            out_specs=pl.BlockSpec((tm, D_MODEL), lambda i, d: (i, 0)),
            scratch_shapes=(
                [pltpu.VMEM((TOP_K, tm * ROW_TILE, LANES), F32)] * GATHER_BUFS
                + [pltpu.SemaphoreType.DMA((GATHER_BUFS,))]),
        ),
        out_shape=jax.ShapeDtypeStruct((t, D_MODEL), F32),
        compiler_params=pltpu.CompilerParams(
            dimension_semantics=("arbitrary",), vmem_limit_bytes=VMEM_LIMIT),
        name="combine",
    )(dest_flat, x1, tw, yb)


ROWTOK_STEPS = 64


def _rowtok_kernel(dest_ref, pad_ref, tok_ref):
    j = pl.program_id(0)
    asg_per = dest_ref.shape[0] // ROWTOK_STEPS
    n_ranges = pad_ref.shape[0] // 2

    @pl.when(j == 0)
    def _():
        for e in range(n_ranges):
            def zero(r, carry):
                tok_ref[r] = 0
                return carry
            lax.fori_loop(pad_ref[e], pad_ref[n_ranges + e], zero, 0)

    base = j * asg_per

    def put(a, carry):
        a = base + a
        tok_ref[dest_ref[a]] = lax.shift_right_logical(a, TOP_K.bit_length() - 1)
        return carry
    lax.fori_loop(0, asg_per, put, 0, unroll=16)


def _row_tokens(dest_flat, pad_bounds, n_rows):
    assert TOP_K & (TOP_K - 1) == 0 and dest_flat.shape[0] % ROWTOK_STEPS == 0
    return pl.pallas_call(
        _rowtok_kernel,
        grid=(ROWTOK_STEPS,),
        in_specs=[pl.BlockSpec(memory_space=pltpu.SMEM), pl.BlockSpec(memory_space=pltpu.SMEM)],
        out_specs=pl.BlockSpec(memory_space=pltpu.SMEM),
        out_shape=jax.ShapeDtypeStruct((n_rows,), jnp.int32),
        compiler_params=pltpu.CompilerParams(dimension_semantics=("arbitrary",)),
        name="rowtok",
    )(dest_flat, pad_bounds)


def _routing_tables(top_idx, pos, counts, t):
    padded = ((counts + MOE_BLOCK - 1) // MOE_BLOCK) * MOE_BLOCK
    cum_padded = jnp.cumsum(padded)
    pstart = cum_padded - padded
    dest = (pstart[top_idx] + pos).astype(jnp.int32)
    n_rows = t * TOP_K + N_EXPERTS * MOE_BLOCK
    n_blocks = n_rows // MOE_BLOCK
    tail = jnp.full((1,), n_rows, jnp.int32)
    pad_bounds = jnp.concatenate([pstart + counts, cum_padded[-1:], cum_padded, tail]).astype(jnp.int32)
    buf_tok = _row_tokens(dest.reshape(-1), pad_bounds, n_rows)
    block_start = jnp.arange(n_blocks, dtype=jnp.int32) * MOE_BLOCK
    block_expert = jnp.minimum(
        jnp.sum((cum_padded[None, :] <= block_start[:, None]).astype(jnp.int32), axis=1), N_EXPERTS - 1)
    block_expert = jnp.concatenate([block_expert, cum_padded[-1:] // MOE_BLOCK]).astype(jnp.int32)
    return dest, buf_tok, block_expert


def kernel(x, ln1_w, w_in, gate_b, q_norm_w, k_norm_w, conv_w, conv_b, dt_bias, a_log, d_skip, ssd_norm_w,
           w_o_attn, w_o_ssd, w_out, ln2_w, w_router, b_router, w_gate_up, b_gate_up, w_down, b_down):
    b, seq, d = x.shape
    t = b * seq
    assert d == D_MODEL and seq % MOBA_BLOCK == 0 and seq % SSD_CHUNK == 0
    assert ln1_w.shape[0] == 1, "single layer"
    x2 = x.reshape(t, d)

    wi = w_in[0]
    col_dt = 3 * ATTN_WIDTH + D_INNER + D_XBC
    w_r = jnp.concatenate(
        [wi[:, :col_dt], wi[:, col_dt + SSD_HEADS:], wi[:, col_dt:col_dt + SSD_HEADS],
         jnp.zeros((d, LANES - SSD_HEADS), wi.dtype)], axis=1).astype(BF16)
    qw2 = jnp.tile(q_norm_w[0], 2)[None, :]
    kw2 = jnp.tile(k_norm_w[0], 2)[None, :]
    pad_g = lambda v: jnp.pad(v.reshape(N_GROUPS, 1, HEADS_PER_GROUP),
                              ((0, 0), (0, 0), (0, LANES - HEADS_PER_GROUP)))
    dtb_g = pad_g(dt_bias[0])
    alog_g = pad_g(a_log[0])
    dskip_e = jnp.repeat(d_skip[0], SSD_HEAD_DIM).reshape(N_GROUPS, 1, GROUP_WIDTH)
    nw_g = ssd_norm_w[0].reshape(N_GROUPS, 1, GROUP_WIDTH)
    wr_hi = w_router[0].astype(BF16)
    wr_lo = (w_router[0] - wr_hi.astype(F32)).astype(BF16)
    zr = jnp.zeros((d, N_EXPERTS), BF16)
    wr_p = jnp.concatenate([jnp.concatenate([wr_hi, wr_lo, zr, zr], axis=1),
                            jnp.concatenate([wr_hi, zr, zr, zr], axis=1)], axis=0)
    br_p = jnp.pad(b_router[0], (0, LANES - N_EXPERTS))[None, :]
    bgu = b_gate_up[0].reshape(N_EXPERTS, 2 * D_FF // GU_PAIR, LANES, 2).transpose(0, 1, 3, 2)
    bgu = bgu.reshape(N_EXPERTS, N_GU, 1, GU_PAIR)
    bd = b_down[0][:, None, :]

    proj = _inproj(x2, ln1_w, w_r)
    proj3 = proj.reshape(b, seq, NP)
    attn = _moba(proj3, qw2, kw2)
    ssd = _ssd(proj3, conv_w[0], conv_b, dtb_g, alog_g, dskip_e, nw_g)
    x1, h2, ti, tw, cnt = _merge(x2, attn.reshape(t, ATTN_WIDTH), ssd.reshape(t, D_INNER), proj, gate_b[0],
                                 w_o_attn[0].astype(BF16), w_o_ssd[0].astype(BF16), w_out[0].astype(BF16),
                                 ln2_w, wr_p, br_p)

    dest, buf_tok, block_expert = _routing_tables(
        ti[:, :TOP_K], ti[:, TOP_K:2 * TOP_K], cnt[0, :N_EXPERTS].astype(jnp.int32), t)
    yb = _experts(block_expert, buf_tok, h2, w_gate_up[0], bgu, w_down[0], bd)
    out = _combine(dest.reshape(-1), x1, tw, yb)
    return out.reshape(b, seq, d)
```

```python
import functools

import jax
import jax.numpy as jnp
from jax import lax
from jax.experimental import pallas as pl
from jax.experimental.pallas import tpu as pltpu

F32 = jnp.float32
BF16 = jnp.bfloat16

D_MODEL = 1024
N_HEADS = 16
HEAD_DIM = 64
ATTN_WIDTH = N_HEADS * HEAD_DIM
MOBA_BLOCK = 256
MOBA_TOPK = 3
D_INNER = 2048
SSD_HEAD_DIM = 64
SSD_HEADS = D_INNER // SSD_HEAD_DIM
N_GROUPS = 4
HEADS_PER_GROUP = SSD_HEADS // N_GROUPS
GROUP_WIDTH = HEADS_PER_GROUP * SSD_HEAD_DIM
D_STATE = 128
CONV_K = 4
SSD_CHUNK = 128
D_XBC = D_INNER + 2 * N_GROUPS * D_STATE
N_EXPERTS = 32
TOP_K = 4
D_FF = D_MODEL
SWIGLU_LIMIT = 7.0
SWIGLU_ALPHA = 1.702
MOE_BLOCK = 256
NORM_EPS = 1e-6
NEG_INF = -1e30

LANES = 128
SUBLANES = 8

P_Q = 0
P_Z = 3 * ATTN_WIDTH
P_XBC = P_Z + D_INNER
P_GATE = P_XBC + D_XBC
P_DT = P_GATE + 2 * D_MODEL
NP = P_DT + LANES
PROJ_TN = 1152
VMEM_LIMIT = 56 * 1024 * 1024


def _split3(v):
    hi = v.astype(BF16)
    r1 = v - hi.astype(F32)
    mid = r1.astype(BF16)
    lo = (r1 - mid.astype(F32)).astype(BF16)
    return hi, mid, lo


def _dot(a, b):
    return jnp.dot(a, b, preferred_element_type=F32)


def _dot_nt(a, b):
    return lax.dot_general(a, b, (((1,), (1,)), ((), ())), preferred_element_type=F32)


ROW_TILE = D_MODEL // LANES


def _store_row_tiles(ref, val):
    rows = val.shape[0]
    for c in range(ROW_TILE):
        ref[pl.ds(c, rows, stride=ROW_TILE), :] = val[:, c * LANES:(c + 1) * LANES]


def _load_row_tiles(ref, rows):
    return jnp.concatenate([ref[pl.ds(c, rows, stride=ROW_TILE), :] for c in range(ROW_TILE)], axis=1)


def _inproj_kernel(x_ref, lnw_ref, w_ref, o_ref, h_ref):
    @pl.when(pl.program_id(1) == 0)
    def _():
        x = x_ref[...]
        ms = jnp.mean(x * x, axis=-1, keepdims=True)
        h_ref[...] = (x * lax.rsqrt(ms + NORM_EPS) * lnw_ref[...]).astype(BF16)

    o_ref[...] = _dot(h_ref[...], w_ref[...]).astype(o_ref.dtype)


def _inproj(x2, lnw, w_r):
    t = x2.shape[0]
    tm = min(1024, t)
    return pl.pallas_call(
        _inproj_kernel,
        grid=(t // tm, NP // PROJ_TN),
        in_specs=[
            pl.BlockSpec((tm, D_MODEL), lambda i, j: (i, 0)),
            pl.BlockSpec((1, D_MODEL), lambda i, j: (0, 0)),
            pl.BlockSpec((D_MODEL, PROJ_TN), lambda i, j: (0, j)),
        ],
        out_specs=pl.BlockSpec((tm, PROJ_TN), lambda i, j: (i, j)),
        out_shape=jax.ShapeDtypeStruct((t, NP), BF16),
        scratch_shapes=[pltpu.VMEM((tm, D_MODEL), BF16)],
        compiler_params=pltpu.CompilerParams(
            dimension_semantics=("parallel", "arbitrary"), vmem_limit_bytes=VMEM_LIMIT),
        name="inproj",
    )(x2, lnw, w_r)


def _moba_kernel(q_ref, k_ref, v_ref, qw_ref, kw_ref, o_ref, *, seq, nb):
    lane = lax.broadcasted_iota(jnp.int32, (1, LANES), 1)
    head0 = lane < HEAD_DIM
    r = lax.broadcasted_iota(jnp.int32, (2 * LANES, LANES), 0) % LANES // HEAD_DIM
    c = lax.broadcasted_iota(jnp.int32, (2 * LANES, LANES), 1) // HEAD_DIM
    avg2 = jnp.where(r == c, 1.0 / HEAD_DIM, 0.0).astype(BF16)

    def qk_norm(t_ref, w_ref):
        t = t_ref[0].astype(F32)
        sq = t * t
        hi = sq.astype(BF16)
        lo = (sq - hi.astype(F32)).astype(BF16)
        ms = _dot(jnp.concatenate([hi, lo], axis=1), avg2)
        return t * lax.rsqrt(ms + NORM_EPS) * w_ref[...]

    qn = qk_norm(q_ref, qw_ref)
    kn = qk_norm(k_ref, kw_ref)
    qs = qn * (HEAD_DIM ** -0.5)
    qs_b = qs.astype(BF16)

    kmean = jnp.mean(kn.reshape(nb, MOBA_BLOCK, LANES), axis=1)
    kmx = jnp.concatenate([jnp.where(head0, kmean, 0.0), jnp.where(head0, 0.0, kmean)], axis=0)
    kmx_hi = kmx.astype(BF16)
    kmx_lo = (kmx - kmx_hi.astype(F32)).astype(BF16)
    st = _dot_nt(jnp.concatenate([kmx_hi, kmx_lo], axis=0), qs_b)
    st = st[:2 * nb] + st[2 * nb:]

    qblk = lax.broadcasted_iota(jnp.int32, (nb, seq), 1) // MOBA_BLOCK
    jrow = lax.broadcasted_iota(jnp.int32, (nb, seq), 0)
    past = jrow < qblk
    bias = []
    for a in range(2):
        sm = jnp.where(past, st[a * nb:(a + 1) * nb], NEG_INF)
        rank = jnp.zeros((nb, seq), jnp.int32)
        for jp in range(nb):
            other = sm[jp:jp + 1, :]
            ahead = (other > sm) | ((other == sm) & (jp < jrow))
            rank = rank + ahead.astype(jnp.int32)
        bias.append(jnp.where(past & (rank >= MOBA_TOPK), NEG_INF, 0.0).astype(F32))
    zpad = jnp.zeros((HEAD_DIM - nb, seq), F32)
    bias_t = jnp.concatenate([bias[1], zpad, bias[0], zpad], axis=0).T

    rblk = lax.broadcasted_iota(jnp.int32, (seq, LANES), 0) // MOBA_BLOCK
    l64 = lax.broadcasted_iota(jnp.int32, (seq, LANES), 1) % HEAD_DIM
    ind = (l64 == rblk).astype(F32)

    q_aug = (jnp.where(head0, qs, bias_t).astype(BF16), jnp.where(head0, bias_t, qs).astype(BF16))
    k_aug = (jnp.where(head0, kn, ind).astype(BF16), jnp.where(head0, ind, kn).astype(BF16))
    v = v_ref[0]
    one = jnp.ones((), BF16)
    v_aug = (jnp.where(head0, v, one), jnp.where(head0, one, v))

    tri = (lax.broadcasted_iota(jnp.int32, (MOBA_BLOCK, MOBA_BLOCK), 0)
           >= lax.broadcasted_iota(jnp.int32, (MOBA_BLOCK, MOBA_BLOCK), 1))
    for i in range(nb):
        lo, hi = i * MOBA_BLOCK, (i + 1) * MOBA_BLOCK
        outs = []
        for a in range(2):
            s = _dot_nt(q_aug[a][lo:hi], k_aug[a][:hi])
            own = jnp.where(tri, s[:, lo:], NEG_INF)
            s = own if i == 0 else jnp.concatenate([s[:, :lo], own], axis=1)
            m = jnp.max(s, axis=1, keepdims=True)
            p = jnp.exp(s - m).astype(BF16)
            o = _dot(p, v_aug[a][:hi])
            outs.append(o / pltpu.roll(o, HEAD_DIM, axis=1))
        o_ref[0, lo:hi, :] = jnp.where(head0, outs[0], outs[1]).astype(o_ref.dtype)


def _moba(proj3, qw2, kw2):
    b, seq, _ = proj3.shape
    nb = seq // MOBA_BLOCK
    npair = ATTN_WIDTH // LANES
    return pl.pallas_call(
        functools.partial(_moba_kernel, seq=seq, nb=nb),
        grid=(b, npair),
        in_specs=[
            pl.BlockSpec((1, seq, LANES), lambda i, j: (i, 0, j)),
            pl.BlockSpec((1, seq, LANES), lambda i, j: (i, 0, npair + j)),
            pl.BlockSpec((1, seq, LANES), lambda i, j: (i, 0, 2 * npair + j)),
            pl.BlockSpec((1, LANES), lambda i, j: (0, 0)),
            pl.BlockSpec((1, LANES), lambda i, j: (0, 0)),
        ],
        out_specs=pl.BlockSpec((1, seq, LANES), lambda i, j: (i, 0, j)),
        out_shape=jax.ShapeDtypeStruct((b, seq, ATTN_WIDTH), BF16),
        compiler_params=pltpu.CompilerParams(
            dimension_semantics=("parallel", "parallel"), vmem_limit_bytes=VMEM_LIMIT),
        name="moba",
    )(proj3, proj3, proj3, qw2, kw2)


def _ssd_kernel(x_ref, b_ref, c_ref, z_ref, dt_ref, cwx_ref, cwb_ref, cwc_ref, cbx_ref, cbb_ref, cbc_ref,
                dtb_ref, alog_ref, dskip_ref, nw_ref, o_ref, st_ref, *, nc):
    g = pl.program_id(1)
    ch = SSD_CHUNK
    st_ref[...] = jnp.zeros_like(st_ref)

    ri = lax.broadcasted_iota(jnp.int32, (LANES, LANES), 0)
    ci = lax.broadcasted_iota(jnp.int32, (LANES, LANES), 1)
    sel_g = ((ri == g * HEADS_PER_GROUP + ci) & (ci < HEADS_PER_GROUP)).astype(BF16)
    neg_a = jnp.where(ci[0:1, :] < HEADS_PER_GROUP, -jnp.exp(alog_ref[0]), 0.0)
    tril_b = ri >= ci
    tril = tril_b.astype(BF16)
    tril3 = jnp.concatenate([tril, tril, tril], axis=1)
    er = lax.broadcasted_iota(jnp.int32, (LANES, GROUP_WIDTH), 0)
    ec = lax.broadcasted_iota(jnp.int32, (LANES, GROUP_WIDTH), 1) // SSD_HEAD_DIM
    e1 = (er == ec).astype(BF16)
    e3 = jnp.concatenate([e1, e1, e1], axis=0)
    lane_h0 = lax.broadcasted_iota(jnp.int32, (1, LANES), 1) < SSD_HEAD_DIM
    pack = 2 * SUBLANES

    def chunk(c, carry):
        r0 = pl.multiple_of(c * ch, ch)
        p0 = pl.multiple_of(jnp.maximum(r0 - pack, 0), pack)
        has_prev = c > 0

        def conv_silu(ref, w_ref, bias_ref):
            cur = ref[0, pl.ds(r0, ch), :].astype(F32)
            prev = ref[0, pl.ds(p0, pack), :].astype(F32)[SUBLANES:]
            prev = jnp.where(has_prev, prev, 0.0)
            cat = jnp.concatenate([prev, cur], axis=0)
            w = w_ref[...]
            out = cur * w[CONV_K - 1:CONV_K, :]
            for k in range(1, CONV_K):
                out = out + pltpu.roll(cat, k, axis=0)[SUBLANES:] * w[CONV_K - 1 - k:CONV_K - k, :]
            out = out + bias_ref[...]
            return out * jax.nn.sigmoid(out)

        xs = conv_silu(x_ref, cwx_ref, cbx_ref)
        bm = conv_silu(b_ref, cwb_ref, cbb_ref)
        cm = conv_silu(c_ref, cwc_ref, cbc_ref)
        _ssd_chunk(xs, bm, cm, dt_ref[0, pl.ds(r0, ch), :], z_ref[0, pl.ds(r0, ch), :], sel_g, neg_a, tril_b,
                   tril3, e3, lane_h0, dtb_ref, dskip_ref, nw_ref, o_ref.at[0, pl.ds(r0, ch), :], st_ref)
        return carry

    lax.fori_loop(0, nc, chunk, 0, unroll=4)


def _ssd_chunk(xs, bm, cm, dt_raw, z_raw, sel_g, neg_a, tril_b, tril3, e3, lane_h0, dtb_ref, dskip_ref, nw_ref,
               o_ref, st_ref):
    ch = SSD_CHUNK
    dtr = _dot(dt_raw, sel_g) + dtb_ref[0]
    dtf = jnp.maximum(dtr, 0.0) + jnp.log1p(jnp.exp(-jnp.abs(dtr)))
    a = dtf * neg_a

    a_cum = _dot(tril3, jnp.concatenate(_split3(a), axis=0))
    a_tot = a_cum[ch - 1:ch, :]

    vals = jnp.concatenate([dtf, jnp.exp(a_cum), jnp.exp(a_tot - a_cum)], axis=0)
    ex = _dot(jnp.concatenate(_split3(vals), axis=1), e3)
    dt_e, eac_e, dec_e = ex[:ch], ex[ch:2 * ch], ex[2 * ch:]
    etot_e = eac_e[ch - 1:ch, :]

    xdt = xs * dt_e
    xdt_b = xdt.astype(BF16)
    bm_b = bm.astype(BF16)
    cm_b = cm.astype(BF16)
    cb = _dot_nt(cm_b, bm_b)
    a_cum_t = a_cum.T
    pieces = []
    for hp in range(HEADS_PER_GROUP // 2):
        xpair = xdt_b[:, hp * LANES:(hp + 1) * LANES]
        yd = []
        for hh in range(2):
            h = 2 * hp + hh
            seg = a_cum[:, h:h + 1] - a_cum_t[h:h + 1, :]
            lmat = jnp.exp(jnp.where(tril_b, seg, -jnp.inf))
            yd.append(_dot((cb * lmat).astype(BF16), xpair))
        pieces.append(jnp.where(lane_h0, yd[0], yd[1]))
    y_diag = jnp.concatenate(pieces, axis=1)

    s_prev = st_ref[...]
    y_off = _dot(cm_b, s_prev.astype(BF16)) * eac_e
    st_ref[...] = etot_e * s_prev + _dot(bm.T.astype(BF16), (xdt * dec_e).astype(BF16))

    y = y_diag + y_off + dskip_ref[0] * xs
    z = z_raw.astype(F32)
    gated = y * (z * jax.nn.sigmoid(z))
    ms = jnp.mean(gated * gated, axis=-1, keepdims=True)
    o_ref[...] = (gated * lax.rsqrt(ms + NORM_EPS) * nw_ref[0]).astype(o_ref.dtype)


def _ssd(proj3, conv_w, conv_b2, dtb_g, alog_g, dskip_e, nw_g):
    b, seq, _ = proj3.shape
    nc = seq // SSD_CHUNK
    gw = GROUP_WIDTH
    xo, bo, co, zo = P_XBC // gw, (P_XBC + D_INNER) // LANES, (P_XBC + D_INNER) // LANES + N_GROUPS, P_Z // gw
    return pl.pallas_call(
        functools.partial(_ssd_kernel, nc=nc),
        grid=(b, N_GROUPS),
        in_specs=[
            pl.BlockSpec((1, seq, gw), lambda i, g: (i, 0, xo + g)),
            pl.BlockSpec((1, seq, LANES), lambda i, g: (i, 0, bo + g)),
            pl.BlockSpec((1, seq, LANES), lambda i, g: (i, 0, co + g)),
            pl.BlockSpec((1, seq, gw), lambda i, g: (i, 0, zo + g)),
            pl.BlockSpec((1, seq, LANES), lambda i, g: (i, 0, P_DT // LANES)),
            pl.BlockSpec((CONV_K, gw), lambda i, g: (0, g)),
            pl.BlockSpec((CONV_K, LANES), lambda i, g: (0, D_INNER // LANES + g)),
            pl.BlockSpec((CONV_K, LANES), lambda i, g: (0, D_INNER // LANES + N_GROUPS + g)),
            pl.BlockSpec((1, gw), lambda i, g: (0, g)),
            pl.BlockSpec((1, LANES), lambda i, g: (0, D_INNER // LANES + g)),
            pl.BlockSpec((1, LANES), lambda i, g: (0, D_INNER // LANES + N_GROUPS + g)),
            pl.BlockSpec((1, 1, LANES), lambda i, g: (g, 0, 0)),
            pl.BlockSpec((1, 1, LANES), lambda i, g: (g, 0, 0)),
            pl.BlockSpec((1, 1, gw), lambda i, g: (g, 0, 0)),
            pl.BlockSpec((1, 1, gw), lambda i, g: (g, 0, 0)),
        ],
        out_specs=pl.BlockSpec((1, seq, gw), lambda i, g: (i, 0, g)),
        out_shape=jax.ShapeDtypeStruct((b, seq, D_INNER), BF16),
        scratch_shapes=[pltpu.VMEM((D_STATE, gw), F32)],
        compiler_params=pltpu.CompilerParams(
            dimension_semantics=("parallel", "parallel"), vmem_limit_bytes=VMEM_LIMIT),
        name="ssd",
    )(proj3, proj3, proj3, proj3, proj3, conv_w, conv_w, conv_w, conv_b2, conv_b2, conv_b2,
      dtb_g, alog_g, dskip_e, nw_g)


def _merge_kernel(x_ref, attn_ref, ssd_ref, gl0_ref, gl1_ref, gb_ref, woa_ref, wos_ref, wout_ref,
                  ln2_ref, wr_ref, br_ref, x1_ref, h2_ref, ti_ref, tw_ref, cnt_ref, run_ref):
    ao = _dot(attn_ref[...], woa_ref[...])
    so = _dot(ssd_ref[...], wos_ref[...])
    g0 = jax.nn.sigmoid(gl0_ref[...].astype(F32) + gb_ref[0:1, :])
    g1 = jax.nn.sigmoid(gl1_ref[...].astype(F32) + gb_ref[1:2, :])
    mixed = (g0 * ao + g1 * so).astype(BF16)
    x1 = x_ref[...] + _dot(mixed, wout_ref[...])
    x1_ref[...] = x1
    ms = jnp.mean(x1 * x1, axis=-1, keepdims=True)
    h2 = x1 * lax.rsqrt(ms + NORM_EPS) * ln2_ref[...]
    _store_row_tiles(h2_ref, h2)

    hh = h2.astype(BF16)
    hl = (h2 - hh.astype(F32)).astype(BF16)
    r = _dot(jnp.concatenate([hh, hl], axis=1), wr_ref[...])
    logits = r + pltpu.roll(r, LANES - N_EXPERTS, axis=1) + br_ref[...]
    lane = lax.broadcasted_iota(jnp.int32, logits.shape, 1)
    lane_f = lane.astype(F32)
    cur = jnp.where(lane < N_EXPERTS, logits, -jnp.inf)
    vals, idxs = [], []
    for _ in range(TOP_K):
        m = jnp.max(cur, axis=1, keepdims=True)
        idx = jnp.min(jnp.where(cur == m, lane_f, float(LANES)), axis=1,
                      keepdims=True).astype(jnp.int32)
        vals.append(m)
        idxs.append(idx)
        cur = jnp.where(lane == idx, -jnp.inf, cur)
    es = [jnp.exp(v - vals[0]) for v in vals]
    den = es[0] + es[1] + es[2] + es[3]

    @pl.when(pl.program_id(0) == 0)
    def _():
        run_ref[...] = jnp.zeros_like(run_ref)

    tm = logits.shape[0]
    multi_hot = jnp.zeros(logits.shape, F32)
    for k in range(TOP_K):
        multi_hot = multi_hot + (lane == idxs[k]).astype(F32)
    earlier = (lax.broadcasted_iota(jnp.int32, (tm, tm), 0)
               > lax.broadcasted_iota(jnp.int32, (tm, tm), 1)).astype(BF16)
    before = run_ref[...] + _dot(earlier, multi_hot.astype(BF16))
    run = run_ref[...] + jnp.sum(multi_hot, axis=0, keepdims=True)
    run_ref[...] = run
    cnt_ref[...] = jnp.broadcast_to(run, cnt_ref.shape)

    ti = jnp.zeros(logits.shape, jnp.int32)
    tw = jnp.zeros(logits.shape, F32)
    for k in range(TOP_K):
        pos = jnp.sum(jnp.where(lane == idxs[k], before, 0.0), axis=1, keepdims=True).astype(jnp.int32)
        ti = jnp.where(lane == k, idxs[k], ti)
        ti = jnp.where(lane == TOP_K + k, pos, ti)
        tw = jnp.where(lane == k, es[k] / den, tw)
    ti_ref[...] = ti
    tw_ref[...] = tw


def _merge(x2, attn2, ssd2, proj, gate_b, woa, wos, wout, ln2, wr, br):
    t = x2.shape[0]
    tm = min(512, t)
    const = lambda i: (0, 0)
    gcol = P_GATE // D_MODEL
    return pl.pallas_call(
        _merge_kernel,
        grid=(t // tm,),
        in_specs=[
            pl.BlockSpec((tm, D_MODEL), lambda i: (i, 0)),
            pl.BlockSpec((tm, ATTN_WIDTH), lambda i: (i, 0)),
            pl.BlockSpec((tm, D_INNER), lambda i: (i, 0)),
            pl.BlockSpec((tm, D_MODEL), lambda i: (i, gcol)),
            pl.BlockSpec((tm, D_MODEL), lambda i: (i, gcol + 1)),
            pl.BlockSpec((2, D_MODEL), const),
            pl.BlockSpec((ATTN_WIDTH, D_MODEL), const),
            pl.BlockSpec((D_INNER, D_MODEL), const),
            pl.BlockSpec((D_MODEL, D_MODEL), const),
            pl.BlockSpec((1, D_MODEL), const),
            pl.BlockSpec((2 * D_MODEL, LANES), const),
            pl.BlockSpec((1, LANES), const),
        ],
        out_specs=[
            pl.BlockSpec((tm, D_MODEL), lambda i: (i, 0)),
            pl.BlockSpec((tm * ROW_TILE, LANES), lambda i: (i, 0)),
            pl.BlockSpec((tm, LANES), lambda i: (i, 0)),
            pl.BlockSpec((tm, LANES), lambda i: (i, 0)),
            pl.BlockSpec((SUBLANES, LANES), const),
        ],
        out_shape=[
            jax.ShapeDtypeStruct((t, D_MODEL), F32),
            jax.ShapeDtypeStruct((t * ROW_TILE, LANES), F32),
            jax.ShapeDtypeStruct((t, LANES), jnp.int32),
            jax.ShapeDtypeStruct((t, LANES), F32),
            jax.ShapeDtypeStruct((SUBLANES, LANES), F32),
        ],
        scratch_shapes=[pltpu.VMEM((1, LANES), F32)],
        compiler_params=pltpu.CompilerParams(
            dimension_semantics=("arbitrary",), vmem_limit_bytes=VMEM_LIMIT),
        name="merge",
    )(x2, attn2, ssd2, proj, proj, gate_b, woa, wos, wout, ln2, wr, br)


GU_PAIR = 2 * LANES
N_GU = 2 * D_FF // GU_PAIR
GATHER_DEPTH = 2
GATHER_BUFS = GATHER_DEPTH + 1


def _expert_kernel(be_ref, tok_ref, h_hbm, wgu_ref, bgu_ref, wd_ref, bd_ref, y_ref, *scratch):
    i = pl.program_id(0)
    nblk = be_ref[pl.num_programs(0)]
    bufs, sem, wgu_s, wd_s = scratch[:GATHER_BUFS], *scratch[GATHER_BUFS:]

    @pl.when(((i == 0) | (be_ref[i] != be_ref[jnp.maximum(i - 1, 0)])) & (i < nblk))
    def _():
        r = lax.broadcasted_iota(jnp.int32, (GU_PAIR, GU_PAIR), 0)
        c = lax.broadcasted_iota(jnp.int32, (GU_PAIR, GU_PAIR), 1)
        perm = (r == jnp.where(c < LANES, 2 * c, 2 * (c - LANES) + 1)).astype(BF16)
        for k in range(N_GU):
            w = wgu_ref[0, :, k * GU_PAIR:(k + 1) * GU_PAIR].astype(BF16)
            wgu_s[k] = _dot(w, perm).astype(BF16)
        wd_s[...] = wd_ref[0].astype(BF16)

    def row_copy(blk, r, par):
        t = tok_ref[blk * MOE_BLOCK + r]
        src = h_hbm.at[pl.ds(pl.multiple_of(t * ROW_TILE, ROW_TILE), ROW_TILE), :]
        dst = bufs[par].at[pl.ds(pl.multiple_of(r * ROW_TILE, ROW_TILE), ROW_TILE), :]
        return pltpu.make_async_copy(src, dst, sem.at[par])

    def wait(blk, par):
        def body(r, carry):
            row_copy(blk, r, par).wait()
            return carry
        lax.fori_loop(0, MOE_BLOCK, body, 0, unroll=8)

    @pl.when(i == 0)
    def _():
        for d in range(GATHER_DEPTH):
            def body(r, carry, d=d):
                row_copy(jnp.minimum(d, nblk - 1), r, d).start()
                return carry
            lax.fori_loop(0, MOE_BLOCK, body, 0, unroll=8)

    @pl.when(i >= nblk)
    def _():
        y_ref[...] = jnp.zeros_like(y_ref)

    def step(par):
        wait(i, par)
        nxt = jnp.minimum(i + GATHER_DEPTH, nblk - 1)
        npar = (par + GATHER_DEPTH) % GATHER_BUFS
        for r in range(MOE_BLOCK):
            row_copy(nxt, r, npar).start(priority=r % 2)

        x = _load_row_tiles(bufs[par], MOE_BLOCK).astype(BF16)
        acts = []
        for k in range(N_GU):
            gu = _dot(x, wgu_s[k]) + bgu_ref[0, k]
            g = jnp.minimum(gu[:, :LANES], SWIGLU_LIMIT)
            u = jnp.clip(gu[:, LANES:], -SWIGLU_LIMIT, SWIGLU_LIMIT)
            acts.append(((u + 1.0) * (g * jax.nn.sigmoid(SWIGLU_ALPHA * g))).astype(BF16))
        _store_row_tiles(y_ref, _dot(jnp.concatenate(acts, axis=1), wd_s[...]) + bd_ref[0])

        @pl.when(i == nblk - 1)
        def _():
            for d in range(1, GATHER_BUFS):
                wait(nxt, (par + d) % GATHER_BUFS)

    for par in range(GATHER_BUFS):
        pl.when((lax.rem(i, GATHER_BUFS) == par) & (i < nblk))(functools.partial(step, par))


def _experts(block_expert, buf_tok, h2, wgu, bgu, wd, bd):
    n_rows = buf_tok.shape[0]
    n_blocks = n_rows // MOE_BLOCK
    wmap = lambda i, be, tok: (be[i], 0, 0)
    return pl.pallas_call(
        _expert_kernel,
        grid_spec=pltpu.PrefetchScalarGridSpec(
            num_scalar_prefetch=2,
            grid=(n_blocks,),
            in_specs=[
                pl.BlockSpec(memory_space=pl.ANY),
                pl.BlockSpec((1, D_MODEL, 2 * D_FF), wmap),
                pl.BlockSpec((1, N_GU, 1, GU_PAIR), lambda i, be, tok: (be[i], 0, 0, 0)),
                pl.BlockSpec((1, D_FF, D_MODEL), wmap),
                pl.BlockSpec((1, 1, D_MODEL), wmap),
            ],
            out_specs=pl.BlockSpec((MOE_BLOCK * ROW_TILE, LANES), lambda i, be, tok: (i, 0)),
            scratch_shapes=(
                [pltpu.VMEM((MOE_BLOCK * ROW_TILE, LANES), F32)] * GATHER_BUFS
                + [pltpu.SemaphoreType.DMA((GATHER_BUFS,)),
                   pltpu.VMEM((N_GU, D_MODEL, GU_PAIR), BF16),
                   pltpu.VMEM((D_FF, D_MODEL), BF16)]),
        ),
        out_shape=jax.ShapeDtypeStruct((n_rows * ROW_TILE, LANES), F32),
        compiler_params=pltpu.CompilerParams(
            dimension_semantics=("arbitrary",), vmem_limit_bytes=VMEM_LIMIT),
        name="experts",
    )(block_expert, buf_tok, h2, wgu, bgu, wd, bd)


COMBINE_TM = 128


def _combine_kernel(dest_ref, x1_ref, tw_ref, y_hbm, o_ref, *scratch):
    i = pl.program_id(0)
    nt = pl.num_programs(0)
    tm = COMBINE_TM
    bufs, sem = scratch[:GATHER_BUFS], scratch[GATHER_BUFS]

    def row_copy(tile, r, k, par):
        d = dest_ref[(tile * tm + r) * TOP_K + k]
        src = y_hbm.at[pl.ds(pl.multiple_of(d * ROW_TILE, ROW_TILE), ROW_TILE), :]
        dst = bufs[par].at[k, pl.ds(pl.multiple_of(r * ROW_TILE, ROW_TILE), ROW_TILE), :]
        return pltpu.make_async_copy(src, dst, sem.at[par])

    def wait(tile, par):
        def body(r, carry):
            for k in range(TOP_K):
                row_copy(tile, r, k, par).wait()
            return carry
        lax.fori_loop(0, tm, body, 0, unroll=2)

    @pl.when(i == 0)
    def _():
        for d in range(GATHER_DEPTH):
            def body(r, carry, d=d):
                for k in range(TOP_K):
                    row_copy(jnp.minimum(d, nt - 1), r, k, d).start()
                return carry
            lax.fori_loop(0, tm, body, 0, unroll=2)

    def step(par):
        wait(i, par)
        nxt = jnp.minimum(i + GATHER_DEPTH, nt - 1)
        npar = (par + GATHER_DEPTH) % GATHER_BUFS
        for r in range(tm):
            for k in range(TOP_K):
                row_copy(nxt, r, k, npar).start(priority=k % 2)

        tw = tw_ref[...]
        gate = [jnp.broadcast_to(tw[:, k:k + 1], (tm, LANES)) for k in range(TOP_K)]
        x1 = x1_ref[...]
        cols = []
        for c in range(ROW_TILE):
            acc = x1[:, c * LANES:(c + 1) * LANES]
            for k in range(TOP_K):
                acc = acc + bufs[par][k, pl.ds(c, tm, stride=ROW_TILE), :] * gate[k]
            cols.append(acc)
        o_ref[...] = jnp.concatenate(cols, axis=1)

        @pl.when(i == nt - 1)
        def _():
            for d in range(1, GATHER_BUFS):
                wait(nxt, (par + d) % GATHER_BUFS)

    for par in range(GATHER_BUFS):
        pl.when(lax.rem(i, GATHER_BUFS) == par)(functools.partial(step, par))


def _combine(dest_flat, x1, tw, yb):
    t = x1.shape[0]
    tm = COMBINE_TM
    return pl.pallas_call(
        _combine_kernel,
        grid_spec=pltpu.PrefetchScalarGridSpec(
            num_scalar_prefetch=1,
            grid=(t // tm,),
            in_specs=[
                pl.BlockSpec((tm, D_MODEL), lambda i, d: (i, 0)),
                pl.BlockSpec((tm, LANES), lambda i, d: (i, 0)),
                pl.BlockSpec(memory_space=pl.ANY),
            ],
            out_specs=pl.BlockSpec((tm, D_MODEL), lambda i, d: (i, 0)),
            scratch_shapes=(
                [pltpu.VMEM((TOP_K, tm * ROW_TILE, LANES), F32)] * GATHER_BUFS
                + [pltpu.SemaphoreType.DMA((GATHER_BUFS,))]),
        ),
        out_shape=jax.ShapeDtypeStruct((t, D_MODEL), F32),
        compiler_params=pltpu.CompilerParams(
            dimension_semantics=("arbitrary",), vmem_limit_bytes=VMEM_LIMIT),
        name="combine",
    )(dest_flat, x1, tw, yb)


ROWTOK_STEPS = 8


def _rowtok_kernel(dest_ref, pad_ref, tok_ref):
    j = pl.program_id(0)
    asg_per = dest_ref.shape[0] // ROWTOK_STEPS
    n_ranges = pad_ref.shape[0] // 2

    @pl.when(j == 0)
    def _():
        for e in range(n_ranges):
            def zero(r, carry):
                tok_ref[r] = 0
                return carry
            lax.fori_loop(pad_ref[e], pad_ref[n_ranges + e], zero, 0)

    base = j * asg_per

    def put(a, carry):
        a = base + a
        tok_ref[dest_ref[a]] = lax.shift_right_logical(a, TOP_K.bit_length() - 1)
        return carry
    lax.fori_loop(0, asg_per, put, 0, unroll=16)


def _row_tokens(dest_flat, pad_bounds, n_rows):
    assert TOP_K & (TOP_K - 1) == 0 and dest_flat.shape[0] % ROWTOK_STEPS == 0
    return pl.pallas_call(
        _rowtok_kernel,
        grid=(ROWTOK_STEPS,),
        in_specs=[pl.BlockSpec(memory_space=pltpu.SMEM), pl.BlockSpec(memory_space=pltpu.SMEM)],
        out_specs=pl.BlockSpec(memory_space=pltpu.SMEM),
        out_shape=jax.ShapeDtypeStruct((n_rows,), jnp.int32),
        compiler_params=pltpu.CompilerParams(dimension_semantics=("arbitrary",)),
        name="rowtok",
    )(dest_flat, pad_bounds)


def _routing_tables(top_idx, pos, counts, t):
    padded = ((counts + MOE_BLOCK - 1) // MOE_BLOCK) * MOE_BLOCK
    cum_padded = jnp.cumsum(padded)
    pstart = cum_padded - padded
    dest = (pstart[top_idx] + pos).astype(jnp.int32)
    n_rows = t * TOP_K + N_EXPERTS * MOE_BLOCK
    n_blocks = n_rows // MOE_BLOCK
    tail = jnp.full((1,), n_rows, jnp.int32)
    pad_bounds = jnp.concatenate([pstart + counts, cum_padded[-1:], cum_padded, tail]).astype(jnp.int32)
    buf_tok = _row_tokens(dest.reshape(-1), pad_bounds, n_rows)
    block_start = jnp.arange(n_blocks, dtype=jnp.int32) * MOE_BLOCK
    block_expert = jnp.minimum(
        jnp.sum((cum_padded[None, :] <= block_start[:, None]).astype(jnp.int32), axis=1), N_EXPERTS - 1)
    block_expert = jnp.concatenate([block_expert, cum_padded[-1:] // MOE_BLOCK]).astype(jnp.int32)
    return dest, buf_tok, block_expert


def kernel(x, ln1_w, w_in, gate_b, q_norm_w, k_norm_w, conv_w, conv_b, dt_bias, a_log, d_skip, ssd_norm_w,
           w_o_attn, w_o_ssd, w_out, ln2_w, w_router, b_router, w_gate_up, b_gate_up, w_down, b_down):
    b, seq, d = x.shape
    t = b * seq
    assert d == D_MODEL and seq % MOBA_BLOCK == 0 and seq % SSD_CHUNK == 0
    assert ln1_w.shape[0] == 1, "single layer"
    x2 = x.reshape(t, d)

    wi = w_in[0]
    col_dt = 3 * ATTN_WIDTH + D_INNER + D_XBC
    w_r = jnp.concatenate(
        [wi[:, :col_dt], wi[:, col_dt + SSD_HEADS:], wi[:, col_dt:col_dt + SSD_HEADS],
         jnp.zeros((d, LANES - SSD_HEADS), wi.dtype)], axis=1).astype(BF16)
    qw2 = jnp.tile(q_norm_w[0], 2)[None, :]
    kw2 = jnp.tile(k_norm_w[0], 2)[None, :]
    pad_g = lambda v: jnp.pad(v.reshape(N_GROUPS, 1, HEADS_PER_GROUP),
                              ((0, 0), (0, 0), (0, LANES - HEADS_PER_GROUP)))
    dtb_g = pad_g(dt_bias[0])
    alog_g = pad_g(a_log[0])
    dskip_e = jnp.repeat(d_skip[0], SSD_HEAD_DIM).reshape(N_GROUPS, 1, GROUP_WIDTH)
    nw_g = ssd_norm_w[0].reshape(N_GROUPS, 1, GROUP_WIDTH)
    wr_hi = w_router[0].astype(BF16)
    wr_lo = (w_router[0] - wr_hi.astype(F32)).astype(BF16)
    zr = jnp.zeros((d, N_EXPERTS), BF16)
    wr_p = jnp.concatenate([jnp.concatenate([wr_hi, wr_lo, zr, zr], axis=1),
                            jnp.concatenate([wr_hi, zr, zr, zr], axis=1)], axis=0)
    br_p = jnp.pad(b_router[0], (0, LANES - N_EXPERTS))[None, :]
    bgu = b_gate_up[0].reshape(N_EXPERTS, 2 * D_FF // GU_PAIR, LANES, 2).transpose(0, 1, 3, 2)
    bgu = bgu.reshape(N_EXPERTS, N_GU, 1, GU_PAIR)
    bd = b_down[0][:, None, :]

    proj = _inproj(x2, ln1_w, w_r)
    proj3 = proj.reshape(b, seq, NP)
    attn = _moba(proj3, qw2, kw2)
    ssd = _ssd(proj3, conv_w[0], conv_b, dtb_g, alog_g, dskip_e, nw_g)
    x1, h2, ti, tw, cnt = _merge(x2, attn.reshape(t, ATTN_WIDTH), ssd.reshape(t, D_INNER), proj, gate_b[0],
                                 w_o_attn[0].astype(BF16), w_o_ssd[0].astype(BF16), w_out[0].astype(BF16),
                                 ln2_w, wr_p, br_p)

    dest, buf_tok, block_expert = _routing_tables(
        ti[:, :TOP_K], ti[:, TOP_K:2 * TOP_K], cnt[0, :N_EXPERTS].astype(jnp.int32), t)
    yb = _experts(block_expert, buf_tok, h2, w_gate_up[0], bgu, w_down[0], bd)
    out = _combine(dest.reshape(-1), x1, tw, yb)
    return out.reshape(b, seq, d)
```
